```python
import math
import jax, jax.numpy as jnp
from jax import lax
import numpy as np

D_MODEL = 1024
BATCH = 32
SEQ = 2048
DEPTH = 1
DEC_BATCH = 8
DEC_SEQ = 16
PAST_LEN = 2048

CHUNK = 64
D_SSM = D_MODEL // 2
SSM_GROUP = 16
SSM_GROUPS = D_SSM // SSM_GROUP
SSM_STATE = 64
D_CONV = D_MODEL // 2
CONV_WIDTH = 31
D_FF = ((8 * D_MODEL // 3 + 127) // 128) * 128
FFN_RES = 0.5
EPS = 1e-6
DT_MIN = 0.001
DT_MAX = 0.1
N_IN = D_SSM + 2 * D_CONV + 2 * D_MODEL

kernel_name = 'streaming_s5_conformer_hybrid_step'


def _rmsnorm(x, g):
    xf = x.astype(jnp.float32)
    y = xf * lax.rsqrt(jnp.mean(xf * xf, axis=-1, keepdims=True) + EPS)
    return (y * g.astype(jnp.float32)).astype(x.dtype)


def _layernorm(x, g, b):
    xf = x.astype(jnp.float32)
    xc = xf - jnp.mean(xf, axis=-1, keepdims=True)
    var = jnp.mean(xc * xc, axis=-1, keepdims=True)
    y = xc * lax.rsqrt(var + EPS) * g.astype(jnp.float32) + b.astype(jnp.float32)
    return y.astype(x.dtype)


def _swiglu(x, w1, w3, w2):
    return (jax.nn.silu(x @ w1) * (x @ w3)) @ w2


def _s5_discretize(lam_re, lam_im, log_step, b_re, b_im):
    f32 = jnp.float32
    lam_re = lam_re.astype(f32)
    lam_im = lam_im.astype(f32)
    dt = jnp.exp(log_step.astype(f32))[:, None]
    mag = jnp.exp(lam_re * dt)
    a_re = mag * jnp.cos(lam_im * dt)
    a_im = mag * jnp.sin(lam_im * dt)
    num_re = a_re - 1.0
    inv_den = 1.0 / (lam_re * lam_re + lam_im * lam_im)
    k_re = (num_re * lam_re + a_im * lam_im) * inv_den
    k_im = (a_im * lam_re - num_re * lam_im) * inv_den
    b_re = b_re.astype(f32)
    b_im = b_im.astype(f32)
    bb_re = k_re[..., None] * b_re - k_im[..., None] * b_im
    bb_im = k_re[..., None] * b_im + k_im[..., None] * b_re
    return a_re, a_im, bb_re, bb_im


def _complex_affine_combine(e1, e2):
    a1r, a1i, b1r, b1i = e1
    a2r, a2i, b2r, b2i = e2
    return (a2r * a1r - a2i * a1i,
            a2r * a1i + a2i * a1r,
            a2r * b1r - a2i * b1i + b2r,
            a2r * b1i + a2i * b1r + b2i)


def _s5(u, p, h0_re, h0_im):
    f32 = jnp.float32
    bsz, length, _ = u.shape
    a_re, a_im, bb_re, bb_im = _s5_discretize(p['ssm_lambda_re'], p['ssm_lambda_im'], p['ssm_log_step'], p['ssm_b_re'], p['ssm_b_im'])
    uf = u.astype(f32)
    ug = uf.reshape(bsz, length, SSM_GROUPS, SSM_GROUP)
    bu_re = jnp.einsum('blgc,gpc->blgp', ug, bb_re)
    bu_im = jnp.einsum('blgc,gpc->blgp', ug, bb_im)
    shape = bu_re.shape
    pr, pi, xr, xi = lax.associative_scan(
        _complex_affine_combine,
        (jnp.broadcast_to(a_re, shape), jnp.broadcast_to(a_im, shape), bu_re, bu_im),
        axis=1)
    if h0_re is not None:
        h0r = h0_re.astype(f32)[:, None]
        h0i = h0_im.astype(f32)[:, None]
        xr, xi = xr + pr * h0r - pi * h0i, xi + pr * h0i + pi * h0r
    c_re = p['ssm_c_re'].astype(f32)
    c_im = p['ssm_c_im'].astype(f32)
    y = jnp.einsum('blgp,gcp->blgc', xr, c_re) - jnp.einsum('blgp,gcp->blgc', xi, c_im)
    y = y.reshape(bsz, length, D_SSM) + p['ssm_d'].astype(f32) * uf
    return y.astype(u.dtype), xr[:, -1], xi[:, -1]


def _mixer(h, p, h0_re, h0_im, conv_buf):
    proj = h @ p['w_in']
    u, conv_a, conv_g, gate_s, gate_c = jnp.split(
        proj, [D_SSM, D_SSM + D_CONV, D_SSM + 2 * D_CONV, D_SSM + 2 * D_CONV + D_MODEL], axis=-1)
    y_ssm, new_re, new_im = _s5(u, p, h0_re, h0_im)
    z = jax.nn.gelu(y_ssm)
    z = z * jax.nn.sigmoid(z @ p['ssm_glu_w'] + p['ssm_glu_b'])
    br_s = z @ p['ssm_out_w']
    v = conv_a * jax.nn.sigmoid(conv_g)
    if conv_buf is None:
        conv_buf = jnp.zeros((v.shape[0], CONV_WIDTH - 1, D_CONV), v.dtype)
    vp = jnp.concatenate([conv_buf.astype(v.dtype), v], axis=1)
    new_buf = vp[:, vp.shape[1] - (CONV_WIDTH - 1):]
    c = lax.conv_general_dilated(
        vp, p['conv_dw_w'][:, None, :].astype(v.dtype), (1,), 'VALID',
        dimension_numbers=('NWC', 'WIO', 'NWC'), feature_group_count=D_CONV)
    c = c + p['conv_dw_b']
    c = jax.nn.silu(_layernorm(c, p['conv_ln_g'], p['conv_ln_b']))
    br_c = c @ p['conv_out_w']
    mix = jax.nn.sigmoid(gate_s) * br_s + jax.nn.sigmoid(gate_c) * br_c
    return mix @ p['w_o'], new_re, new_im, new_buf


def _layer(x, p, h0_re, h0_im, conv_buf):
    x = x + FFN_RES * _swiglu(_rmsnorm(x, p['ffn1_norm']), p['ffn1_w1'], p['ffn1_w3'], p['ffn1_w2'])
    m, new_re, new_im, new_buf = _mixer(_rmsnorm(x, p['mix_norm']), p, h0_re, h0_im, conv_buf)
    x = x + m
    x = x + FFN_RES * _swiglu(_rmsnorm(x, p['ffn2_norm']), p['ffn2_w1'], p['ffn2_w3'], p['ffn2_w2'])
    return x, new_re, new_im, new_buf


def _trunk(x, layer_params, final_norm, ssm_re, ssm_im, conv_cache):
    res_re, res_im, res_buf = [], [], []
    for l in range(DEPTH):
        p = {k: v[l] for k, v in layer_params.items()}
        h0r = None if ssm_re is None else ssm_re[l]
        h0i = None if ssm_im is None else ssm_im[l]
        cb = None if conv_cache is None else conv_cache[l]
        x, nr, ni, nb = _layer(x, p, h0r, h0i, cb)
        res_re.append(nr)
        res_im.append(ni)
        res_buf.append(nb)
    return _rmsnorm(x, final_norm), jnp.stack(res_re), jnp.stack(res_im), jnp.stack(res_buf)


def setup_inputs(seed: int = 0) -> dict:
    key = jax.random.key(seed)
    ks = jax.random.split(key, 40)
    f32 = jnp.float32

    def nrm(k, shape, scale):
        return jax.random.normal(k, shape, f32) * scale

    def gain(k, shape):
        return 1.0 + 0.01 * jax.random.normal(k, shape, f32)

    G, P, Hc = SSM_GROUPS, SSM_STATE, SSM_GROUP
    lam_im_base = jnp.pi * jnp.arange(P, dtype=f32)
    return {
        'x_prompt': nrm(ks[0], (BATCH, SEQ, D_MODEL), 1.0),
        'x_sample': nrm(ks[1], (DEC_BATCH, DEC_SEQ, D_MODEL), 1.0),
        'state_ssm_re': nrm(ks[2], (DEPTH, DEC_BATCH, G, P), 0.1),
        'state_ssm_im': nrm(ks[3], (DEPTH, DEC_BATCH, G, P), 0.1),
        'cache_conv': nrm(ks[4], (DEPTH, DEC_BATCH, CONV_WIDTH - 1, D_CONV), 0.5),
        'ffn1_norm': gain(ks[5], (DEPTH, D_MODEL)),
        'ffn1_w1': nrm(ks[6], (DEPTH, D_MODEL, D_FF), D_MODEL ** -0.5),
        'ffn1_w3': nrm(ks[7], (DEPTH, D_MODEL, D_FF), D_MODEL ** -0.5),
        'ffn1_w2': nrm(ks[8], (DEPTH, D_FF, D_MODEL), D_FF ** -0.5),
        'mix_norm': gain(ks[9], (DEPTH, D_MODEL)),
        'w_in': nrm(ks[10], (DEPTH, D_MODEL, N_IN), D_MODEL ** -0.5),
        'ssm_lambda_re': -0.5 + nrm(ks[11], (DEPTH, G, P), 0.01),
        'ssm_lambda_im': lam_im_base + nrm(ks[12], (DEPTH, G, P), 0.01),
        'ssm_log_step': jax.random.uniform(ks[13], (DEPTH, G), f32, math.log(DT_MIN), math.log(DT_MAX)),
        'ssm_b_re': nrm(ks[14], (DEPTH, G, P, Hc), (2 * Hc) ** -0.5),
        'ssm_b_im': nrm(ks[15], (DEPTH, G, P, Hc), (2 * Hc) ** -0.5),
        'ssm_c_re': nrm(ks[16], (DEPTH, G, Hc, P), P ** -0.5),
        'ssm_c_im': nrm(ks[17], (DEPTH, G, Hc, P), P ** -0.5),
        'ssm_d': nrm(ks[18], (DEPTH, D_SSM), 1.0),
        'ssm_glu_w': nrm(ks[19], (DEPTH, D_SSM, D_SSM), D_SSM ** -0.5),
        'ssm_glu_b': nrm(ks[20], (DEPTH, D_SSM), 0.01),
        'ssm_out_w': nrm(ks[21], (DEPTH, D_SSM, D_MODEL), D_SSM ** -0.5),
        'conv_dw_w': nrm(ks[22], (DEPTH, CONV_WIDTH, D_CONV), CONV_WIDTH ** -0.5),
        'conv_dw_b': nrm(ks[23], (DEPTH, D_CONV), 0.01),
        'conv_ln_g': gain(ks[24], (DEPTH, D_CONV)),
        'conv_ln_b': nrm(ks[25], (DEPTH, D_CONV), 0.01),
        'conv_out_w': nrm(ks[26], (DEPTH, D_CONV, D_MODEL), D_CONV ** -0.5),
        'w_o': nrm(ks[27], (DEPTH, D_MODEL, D_MODEL), D_MODEL ** -0.5),
        'ffn2_norm': gain(ks[28], (DEPTH, D_MODEL)),
        'ffn2_w1': nrm(ks[29], (DEPTH, D_MODEL, D_FF), D_MODEL ** -0.5),
        'ffn2_w3': nrm(ks[30], (DEPTH, D_MODEL, D_FF), D_MODEL ** -0.5),
        'ffn2_w2': nrm(ks[31], (DEPTH, D_FF, D_MODEL), D_FF ** -0.5),
        'final_norm': gain(ks[32], (D_MODEL,)),
    }


def reference(x_prompt, x_sample, state_ssm_re, state_ssm_im, cache_conv,
              ffn1_norm, ffn1_w1, ffn1_w3, ffn1_w2, mix_norm, w_in,
              ssm_lambda_re, ssm_lambda_im, ssm_log_step, ssm_b_re, ssm_b_im,
              ssm_c_re, ssm_c_im, ssm_d, ssm_glu_w, ssm_glu_b, ssm_out_w,
              conv_dw_w, conv_dw_b, conv_ln_g, conv_ln_b, conv_out_w, w_o,
              ffn2_norm, ffn2_w1, ffn2_w3, ffn2_w2, final_norm):
    layer_params = dict(
        ffn1_norm=ffn1_norm, ffn1_w1=ffn1_w1, ffn1_w3=ffn1_w3, ffn1_w2=ffn1_w2,
        mix_norm=mix_norm, w_in=w_in,
        ssm_lambda_re=ssm_lambda_re, ssm_lambda_im=ssm_lambda_im, ssm_log_step=ssm_log_step,
        ssm_b_re=ssm_b_re, ssm_b_im=ssm_b_im, ssm_c_re=ssm_c_re, ssm_c_im=ssm_c_im,
        ssm_d=ssm_d, ssm_glu_w=ssm_glu_w, ssm_glu_b=ssm_glu_b, ssm_out_w=ssm_out_w,
        conv_dw_w=conv_dw_w, conv_dw_b=conv_dw_b, conv_ln_g=conv_ln_g, conv_ln_b=conv_ln_b,
        conv_out_w=conv_out_w, w_o=w_o,
        ffn2_norm=ffn2_norm, ffn2_w1=ffn2_w1, ffn2_w3=ffn2_w3, ffn2_w2=ffn2_w2)
    y_prompt, new_ssm_re_prompt, new_ssm_im_prompt, new_conv_prompt = _trunk(
        x_prompt, layer_params, final_norm, None, None, None)
    y_sample, new_ssm_re_sample, new_ssm_im_sample, new_conv_sample = _trunk(
        x_sample, layer_params, final_norm, state_ssm_re, state_ssm_im, cache_conv)
    return (y_prompt, y_sample, new_ssm_re_prompt, new_ssm_im_prompt, new_conv_prompt,
            new_ssm_re_sample, new_ssm_im_sample, new_conv_sample)
```

```python
import functools
import math

import jax
import jax.numpy as jnp
from jax import lax
from jax.experimental import pallas as pl
from jax.experimental.pallas import tpu as pltpu

D_MODEL = 1024
D_SSM = 512
SSM_GROUP = 16
SSM_GROUPS = 32
SSM_STATE = 64
D_CONV = 512
CONV_WIDTH = 31
D_FF = 2816
FFN_RES = 0.5
EPS = 1e-6

SUBLANES = 8
MXU_DIM = 256
VMEM_BYTES = 64 * 1024 * 1024

N_STATE = SSM_GROUPS * SSM_STATE
SSM_HALVES = D_SSM // MXU_DIM
HALF_STATE = N_STATE // SSM_HALVES
SCAN_LANES = 512
HIST = CONV_WIDTH - 1

F32 = jnp.float32
BF16 = jnp.bfloat16


def _sigmoid(x):
    return 1.0 / (1.0 + jnp.exp(-x))


def _rms(x, g):
    return x * lax.rsqrt(jnp.mean(x * x, axis=-1, keepdims=True) + EPS) * g


def _dot(a, b):
    return jnp.dot(a, b, preferred_element_type=F32)


def _const_spec(shape):
    return pl.BlockSpec(shape, lambda *_: (0,) * len(shape), pipeline_mode=pl.Buffered(1))


def _discretize_kernel(lre_ref, lim_ref, dt_ref, bre_ref, bim_ref, are_ref, aim_ref, bbre_ref, bbim_ref):
    lre = lre_ref[...]
    lim = lim_ref[...]
    dt = jnp.exp(dt_ref[...])
    mag = jnp.exp(lre * dt)
    a_re = mag * jnp.cos(lim * dt)
    a_im = mag * jnp.sin(lim * dt)
    num_re = a_re - 1.0
    inv_den = 1.0 / (lre * lre + lim * lim)
    k_re = (num_re * lre + a_im * lim) * inv_den
    k_im = (a_im * lre - num_re * lim) * inv_den
    are_ref[...] = a_re
    aim_ref[...] = a_im
    bre = bre_ref[...]
    bim = bim_ref[...]
    bbre_ref[...] = k_re * bre - k_im * bim
    bbim_ref[...] = k_re * bim + k_im * bre


def _discretize(lam_re, lam_im, log_step, b_re, b_im):
    lre = lam_re.reshape(1, N_STATE).astype(F32)
    lim = lam_im.reshape(1, N_STATE).astype(F32)
    dt = jnp.repeat(log_step.astype(F32), SSM_STATE).reshape(1, N_STATE)
    bre = b_re.astype(F32).transpose(2, 0, 1).reshape(SSM_GROUP, N_STATE)
    bim = b_im.astype(F32).transpose(2, 0, 1).reshape(SSM_GROUP, N_STATE)
    row = jax.ShapeDtypeStruct((1, N_STATE), F32)
    mat = jax.ShapeDtypeStruct((SSM_GROUP, N_STATE), F32)
    return pl.pallas_call(_discretize_kernel, out_shape=(row, row, mat, mat), name="s5_discretize")(
        lre, lim, dt, bre, bim)


def _block_diag_b(bbt_re, bbt_im):
    gph = SSM_GROUPS // SSM_HALVES
    eye = jnp.eye(gph, dtype=F32)

    def one(bbt):
        b = bbt.reshape(SSM_GROUP, SSM_HALVES, gph, SSM_STATE)
        m = jnp.einsum("chgp,gk->hgckp", b, eye)
        return m.reshape(SSM_HALVES, MXU_DIM, HALF_STATE)

    return jnp.concatenate([one(bbt_re), one(bbt_im)], axis=-1).astype(BF16)


def _block_diag_c(c_re, c_im):
    gph = SSM_GROUPS // SSM_HALVES
    eye = jnp.eye(gph, dtype=F32)

    def one(c):
        cc = c.astype(F32).reshape(SSM_HALVES, gph, SSM_GROUP, SSM_STATE)
        m = jnp.einsum("hgcp,gk->hgpkc", cc, eye)
        return m.reshape(SSM_HALVES, HALF_STATE, MXU_DIM)

    return jnp.concatenate([one(c_re), -one(c_im)], axis=1).astype(BF16)


def _ffn_kernel(x_ref, g_ref, w1_ref, w3_ref, w2_ref, gf_ref, o_ref, *, ff_chunk, final_norm):
    x = x_ref[...]
    h = _rms(x, g_ref[...]).astype(BF16)
    acc = jnp.zeros(x.shape, F32)
    for c in range(D_FF // ff_chunk):
        cols = slice(c * ff_chunk, (c + 1) * ff_chunk)
        a = _dot(h, w1_ref[:, cols])
        b = _dot(h, w3_ref[:, cols])
        gated = (a * _sigmoid(a) * b).astype(BF16)
        acc = acc + _dot(gated, w2_ref[cols, :])
    y = x + FFN_RES * acc
    if final_norm:
        y = _rms(y, gf_ref[...])
    o_ref[...] = y


def _ffn(x, norm_g, w1, w3, w2, final_g, *, nb, length, tm, to_time_major):
    batch_major = pl.BlockSpec((None, tm, D_MODEL), lambda b, i: (b, i, 0))
    time_major = pl.BlockSpec((tm, D_MODEL), lambda b, i: (i, b))
    if to_time_major:
        in_spec, out_spec = batch_major, time_major
        out_shape = jax.ShapeDtypeStruct((length, nb * D_MODEL), F32)
    else:
        in_spec, out_spec = time_major, batch_major
        out_shape = jax.ShapeDtypeStruct((nb, length, D_MODEL), F32)
    weight_bytes = 3 * D_MODEL * D_FF * 2
    block_bytes = tm * D_MODEL * 4
    vmem = weight_bytes + 4 * block_bytes + 8 * block_bytes + (8 << 20)
    return pl.pallas_call(
        functools.partial(_ffn_kernel, ff_chunk=MXU_DIM, final_norm=not to_time_major),
        grid=(nb, length // tm),
        in_specs=[in_spec, _const_spec((1, D_MODEL)), _const_spec((D_MODEL, D_FF)),
                  _const_spec((D_MODEL, D_FF)), _const_spec((D_FF, D_MODEL)), _const_spec((1, D_MODEL))],
        out_specs=out_spec,
        out_shape=out_shape,
        compiler_params=pltpu.CompilerParams(
            dimension_semantics=("arbitrary", "arbitrary"), vmem_limit_bytes=min(vmem, VMEM_BYTES - (6 << 20))),
        name="ffn_to_time_major" if to_time_major else "ffn_final",
    )(x, norm_g, w1, w3, w2, final_g)


def _mixer_kernel(x_ref, gn_ref, win_ref, are_ref, aim_ref, bm_ref, cm_ref, d_ref, gluw_ref, glub_ref, sow_ref,
                  cw_ref, cb_ref, lng_ref, lnb_ref, cow_ref, wo_ref, h0_ref, cache_ref,
                  o_ref, st_ref, cbuf_ref,
                  hb_scr, u_scr, bu_scr, vp_scr, cn_scr, mix_scr, *, nb, q, conv_rows):
    step = pl.program_id(0)
    rows = nb * q
    hist_rows = HIST * nb

    @pl.when(step == 0)
    def _():
        st_ref[...] = h0_ref[...]
        vp_scr[0:hist_rows, :] = cache_ref[...]

    x = x_ref[...]
    hb_scr[...] = _rms(x, gn_ref[...]).astype(BF16)

    conv_a = _dot(hb_scr[...], win_ref[:, D_SSM:D_SSM + D_CONV])
    conv_g = _dot(hb_scr[...], win_ref[:, D_SSM + D_CONV:D_SSM + 2 * D_CONV])
    vp_scr[hist_rows:hist_rows + rows, :] = conv_a * _sigmoid(conv_g)

    def conv_chunk(i, _):
        r0 = pl.multiple_of(i * conv_rows, conv_rows)
        acc = jnp.zeros((conv_rows, D_CONV), F32) + cb_ref[...]
        for k in range(CONV_WIDTH):
            acc = acc + cw_ref[k:k + 1, :] * vp_scr[pl.ds(r0 + k * nb, conv_rows), :]
        mu = jnp.mean(acc, axis=-1, keepdims=True)
        xc = acc - mu
        var = jnp.mean(xc * xc, axis=-1, keepdims=True)
        ln = xc * lax.rsqrt(var + EPS) * lng_ref[...] + lnb_ref[...]
        cn_scr[pl.ds(r0, conv_rows), :] = (ln * _sigmoid(ln)).astype(BF16)
        return 0

    lax.fori_loop(0, rows // conv_rows, conv_chunk, 0)

    @pl.when(step == pl.num_programs(0) - 1)
    def _():
        cbuf_ref[...] = vp_scr[rows:rows + hist_rows, :]

    for r in range(0, hist_rows, nb):
        vp_scr[r:r + nb, :] = vp_scr[rows + r:rows + r + nb, :]

    u_scr[...] = _dot(hb_scr[...], win_ref[:, 0:D_SSM])
    for h in range(SSM_HALVES):
        ub = u_scr[:, h * MXU_DIM:(h + 1) * MXU_DIM].astype(BF16)
        bu_scr[:, 2 * h * HALF_STATE:2 * (h + 1) * HALF_STATE] = _dot(ub, bm_ref[h])

    groups = nb // SUBLANES
    for h in range(SSM_HALVES):
        for j in range(HALF_STATE // SCAN_LANES):
            re_cols = slice(2 * h * HALF_STATE + j * SCAN_LANES, 2 * h * HALF_STATE + (j + 1) * SCAN_LANES)
            im_cols = slice(re_cols.start + HALF_STATE, re_cols.stop + HALF_STATE)
            a_cols = slice(h * HALF_STATE + j * SCAN_LANES, h * HALF_STATE + (j + 1) * SCAN_LANES)
            ar = jnp.broadcast_to(are_ref[:, a_cols], (SUBLANES, SCAN_LANES))
            ai = jnp.broadcast_to(aim_ref[:, a_cols], (SUBLANES, SCAN_LANES))

            def scan_step(t, carry, re_cols=re_cols, im_cols=im_cols, ar=ar, ai=ai):
                base = pl.multiple_of(t * nb, SUBLANES)
                out = []
                for bg in range(groups):
                    sr, si = carry[bg]
                    r0 = base + bg * SUBLANES
                    nr = ar * sr - ai * si + bu_scr[pl.ds(r0, SUBLANES), re_cols]
                    ni = ar * si + ai * sr + bu_scr[pl.ds(r0, SUBLANES), im_cols]
                    bu_scr[pl.ds(r0, SUBLANES), re_cols] = nr
                    bu_scr[pl.ds(r0, SUBLANES), im_cols] = ni
                    out.append((nr, ni))
                return tuple(out)

            init = tuple((st_ref[bg * SUBLANES:(bg + 1) * SUBLANES, re_cols],
                          st_ref[bg * SUBLANES:(bg + 1) * SUBLANES, im_cols]) for bg in range(groups))
            final = lax.fori_loop(0, q, scan_step, init)
            for bg in range(groups):
                st_ref[bg * SUBLANES:(bg + 1) * SUBLANES, re_cols] = final[bg][0]
                st_ref[bg * SUBLANES:(bg + 1) * SUBLANES, im_cols] = final[bg][1]

    y = jnp.concatenate(
        [_dot(bu_scr[:, 2 * h * HALF_STATE:2 * (h + 1) * HALF_STATE].astype(BF16), cm_ref[h])
         for h in range(SSM_HALVES)], axis=-1)
    y = y + d_ref[...] * u_scr[...]
    z = 0.5 * y * (1.0 + jnp.tanh(math.sqrt(2.0 / math.pi) * (y + 0.044715 * (y * y * y))))
    z = z * _sigmoid(_dot(z.astype(BF16), gluw_ref[...]) + glub_ref[...])
    zb = z.astype(BF16)

    g0 = D_SSM + 2 * D_CONV
    for c in range(D_MODEL // MXU_DIM):
        cols = slice(c * MXU_DIM, (c + 1) * MXU_DIM)
        gate_s = _dot(hb_scr[...], win_ref[:, g0 + c * MXU_DIM:g0 + (c + 1) * MXU_DIM])
        gate_c = _dot(hb_scr[...], win_ref[:, g0 + D_MODEL + c * MXU_DIM:g0 + D_MODEL + (c + 1) * MXU_DIM])
        br_s = _dot(zb, sow_ref[:, cols])
        br_c = _dot(cn_scr[...], cow_ref[:, cols])
        mix_scr[:, cols] = (_sigmoid(gate_s) * br_s + _sigmoid(gate_c) * br_c).astype(BF16)
    o_ref[...] = x_ref[...] + _dot(mix_scr[...], wo_ref[...])


def _mixer(x_tm, p, h0, cache_tm, *, nb, length, q):
    rows = nb * q
    conv_rows = min(64, rows)
    n_in = D_SSM + 2 * D_CONV + 2 * D_MODEL
    consts = [
        (p["mix_norm"], (1, D_MODEL)), (p["w_in"], (D_MODEL, n_in)),
        (p["a_re"], (1, N_STATE)), (p["a_im"], (1, N_STATE)),
        (p["bm"], (SSM_HALVES, MXU_DIM, 2 * HALF_STATE)), (p["cm"], (SSM_HALVES, 2 * HALF_STATE, MXU_DIM)),
        (p["ssm_d"], (1, D_SSM)), (p["ssm_glu_w"], (D_SSM, D_SSM)), (p["ssm_glu_b"], (1, D_SSM)),
        (p["ssm_out_w"], (D_SSM, D_MODEL)),
        (p["conv_dw_w"], (CONV_WIDTH, D_CONV)), (p["conv_dw_b"], (1, D_CONV)),
        (p["conv_ln_g"], (1, D_CONV)), (p["conv_ln_b"], (1, D_CONV)), (p["conv_out_w"], (D_CONV, D_MODEL)),
        (p["w_o"], (D_MODEL, D_MODEL)),
        (h0, (nb, 2 * N_STATE)), (cache_tm, (HIST * nb, D_CONV)),
    ]
    x_spec = pl.BlockSpec((rows, D_MODEL), lambda i: (i, 0))
    scratch = [
        pltpu.VMEM((rows, D_MODEL), BF16),
        pltpu.VMEM((rows, D_SSM), F32),
        pltpu.VMEM((rows, 2 * N_STATE), F32),
        pltpu.VMEM(((HIST + q) * nb, D_CONV), F32),
        pltpu.VMEM((rows, D_CONV), BF16),
        pltpu.VMEM((rows, D_MODEL), BF16),
    ]
    const_bytes = sum(math.prod(s) * a.dtype.itemsize for a, s in consts)
    scratch_bytes = (rows * D_MODEL * 2 * 2 + rows * D_SSM * 4 + rows * 2 * N_STATE * 4
                     + (HIST + q) * nb * D_CONV * 4 + rows * D_CONV * 2)
    block_bytes = rows * D_MODEL * 4
    vmem = const_bytes + scratch_bytes + 4 * block_bytes + 10 * block_bytes + (8 << 20)
    return pl.pallas_call(
        functools.partial(_mixer_kernel, nb=nb, q=q, conv_rows=conv_rows),
        grid=(length // q,),
        in_specs=[x_spec] + [_const_spec(s) for _, s in consts],
        out_specs=(x_spec, pl.BlockSpec((nb, 2 * N_STATE), lambda i: (0, 0)),
                   pl.BlockSpec((HIST * nb, D_CONV), lambda i: (0, 0))),
        out_shape=(jax.ShapeDtypeStruct((length * nb, D_MODEL), F32),
                   jax.ShapeDtypeStruct((nb, 2 * N_STATE), F32),
                   jax.ShapeDtypeStruct((HIST * nb, D_CONV), F32)),
        scratch_shapes=scratch,
        compiler_params=pltpu.CompilerParams(
            dimension_semantics=("arbitrary",), vmem_limit_bytes=min(vmem, VMEM_BYTES - (6 << 20))),
        name="mixer",
    )(x_tm, *[a for a, _ in consts])


def _trunk(x, p, final_g, h0, cache_tm, *, tm, q):
    nb, length, _ = x.shape
    x1 = _ffn(x, p["ffn1_norm"], p["ffn1_w1"], p["ffn1_w3"], p["ffn1_w2"], final_g,
              nb=nb, length=length, tm=tm, to_time_major=True)
    x2, state, conv_tail = _mixer(x1.reshape(length * nb, D_MODEL), p, h0, cache_tm, nb=nb, length=length, q=q)
    y = _ffn(x2.reshape(length, nb * D_MODEL), p["ffn2_norm"], p["ffn2_w1"], p["ffn2_w3"], p["ffn2_w2"], final_g,
             nb=nb, length=length, tm=tm, to_time_major=False)
    st = state.reshape(nb, SSM_HALVES, 2, SSM_GROUPS // SSM_HALVES, SSM_STATE)
    new_re = st[:, :, 0].reshape(1, nb, SSM_GROUPS, SSM_STATE)
    new_im = st[:, :, 1].reshape(1, nb, SSM_GROUPS, SSM_STATE)
    new_conv = conv_tail.reshape(HIST, nb, D_CONV).transpose(1, 0, 2)[None]
    return y, new_re, new_im, new_conv


def _state_rows(re, im):
    nb = re.shape[0]
    gph = SSM_GROUPS // SSM_HALVES
    st = jnp.stack([re.reshape(nb, SSM_HALVES, gph, SSM_STATE), im.reshape(nb, SSM_HALVES, gph, SSM_STATE)], axis=2)
    return st.reshape(nb, 2 * N_STATE).astype(F32)


def kernel(x_prompt, x_sample, state_ssm_re, state_ssm_im, cache_conv, ffn1_norm, ffn1_w1, ffn1_w3, ffn1_w2, mix_norm, w_in, ssm_lambda_re, ssm_lambda_im, ssm_log_step, ssm_b_re, ssm_b_im, ssm_c_re, ssm_c_im, ssm_d, ssm_glu_w, ssm_glu_b, ssm_out_w, conv_dw_w, conv_dw_b, conv_ln_g, conv_ln_b, conv_out_w, w_o, ffn2_norm, ffn2_w1, ffn2_w3, ffn2_w2, final_norm):
    a_re, a_im, bbt_re, bbt_im = _discretize(ssm_lambda_re[0], ssm_lambda_im[0], ssm_log_step[0],
                                             ssm_b_re[0], ssm_b_im[0])
    row = lambda v: v.reshape(1, -1).astype(F32)
    p = dict(
        ffn1_norm=row(ffn1_norm[0]), ffn1_w1=ffn1_w1[0].astype(BF16), ffn1_w3=ffn1_w3[0].astype(BF16),
        ffn1_w2=ffn1_w2[0].astype(BF16),
        mix_norm=row(mix_norm[0]), w_in=w_in[0].astype(BF16),
        a_re=a_re, a_im=a_im, bm=_block_diag_b(bbt_re, bbt_im), cm=_block_diag_c(ssm_c_re[0], ssm_c_im[0]),
        ssm_d=row(ssm_d[0]), ssm_glu_w=ssm_glu_w[0].astype(BF16), ssm_glu_b=row(ssm_glu_b[0]),
        ssm_out_w=ssm_out_w[0].astype(BF16),
        conv_dw_w=conv_dw_w[0].astype(F32), conv_dw_b=row(conv_dw_b[0]),
        conv_ln_g=row(conv_ln_g[0]), conv_ln_b=row(conv_ln_b[0]), conv_out_w=conv_out_w[0].astype(BF16),
        w_o=w_o[0].astype(BF16),
        ffn2_norm=row(ffn2_norm[0]), ffn2_w1=ffn2_w1[0].astype(BF16), ffn2_w3=ffn2_w3[0].astype(BF16),
        ffn2_w2=ffn2_w2[0].astype(BF16),
    )
    final_g = row(final_norm)

    nb_p, len_p, _ = x_prompt.shape
    y_p, re_p, im_p, conv_p = _trunk(
        x_prompt, p, final_g, jnp.zeros((nb_p, 2 * N_STATE), F32), jnp.zeros((HIST * nb_p, D_CONV), F32),
        tm=512, q=16)

    nb_s, len_s, _ = x_sample.shape
    cache_tm = cache_conv[0].astype(F32).transpose(1, 0, 2).reshape(HIST * nb_s, D_CONV)
    y_s, re_s, im_s, conv_s = _trunk(
        x_sample, p, final_g, _state_rows(state_ssm_re[0], state_ssm_im[0]), cache_tm, tm=len_s, q=len_s)

    return (y_p, y_s, re_p, im_p, conv_p, re_s, im_s, conv_s)
```

```python
import functools
import math

import jax
import jax.numpy as jnp
from jax import lax
from jax.experimental import pallas as pl
from jax.experimental.pallas import tpu as pltpu

D_MODEL = 1024
D_SSM = 512
SSM_GROUP = 16
SSM_GROUPS = 32
SSM_STATE = 64
D_CONV = 512
CONV_WIDTH = 31
D_FF = 2816
FFN_RES = 0.5
EPS = 1e-6

SUBLANES = 8
LANES = 128
MXU_DIM = 256
VMEM_BYTES = 64 * 1024 * 1024

N_STATE = SSM_GROUPS * SSM_STATE
SSM_HALVES = D_SSM // MXU_DIM
HALF_STATE = N_STATE // SSM_HALVES
SCAN_LANES = 512
SCAN_T = 4
CONV_ROWS = 64
HIST = CONV_WIDTH - 1
N_IN = D_SSM + 2 * D_CONV + 2 * D_MODEL
GATE0 = D_SSM + 2 * D_CONV

F32 = jnp.float32
BF16 = jnp.bfloat16


def _sigmoid(x):
    return 1.0 / (1.0 + jnp.exp(-x))


def _rms(x, g):
    return x * lax.rsqrt(jnp.mean(x * x, axis=-1, keepdims=True) + EPS) * g


def _dot(a, b):
    return jnp.dot(a, b, preferred_element_type=F32)


def _const_spec(shape):
    return pl.BlockSpec(shape, lambda *_: (0,) * len(shape), pipeline_mode=pl.Buffered(1))


def _discretize_kernel(lre_ref, lim_ref, dt_ref, bre_ref, bim_ref, are_ref, aim_ref, bbre_ref, bbim_ref):
    lre = lre_ref[...]
    lim = lim_ref[...]
    dt = jnp.exp(dt_ref[...])
    mag = jnp.exp(lre * dt)
    a_re = mag * jnp.cos(lim * dt)
    a_im = mag * jnp.sin(lim * dt)
    num_re = a_re - 1.0
    inv_den = 1.0 / (lre * lre + lim * lim)
    k_re = (num_re * lre + a_im * lim) * inv_den
    k_im = (a_im * lre - num_re * lim) * inv_den
    are_ref[...] = a_re
    aim_ref[...] = a_im
    bre = bre_ref[...]
    bim = bim_ref[...]
    bbre_ref[...] = k_re * bre - k_im * bim
    bbim_ref[...] = k_re * bim + k_im * bre


def _discretize(lam_re, lam_im, log_step, b_re, b_im):
    lre = lam_re.reshape(1, N_STATE).astype(F32)
    lim = lam_im.reshape(1, N_STATE).astype(F32)
    dt = jnp.repeat(log_step.astype(F32), SSM_STATE).reshape(1, N_STATE)
    bre = b_re.astype(F32).transpose(2, 0, 1).reshape(SSM_GROUP, N_STATE)
    bim = b_im.astype(F32).transpose(2, 0, 1).reshape(SSM_GROUP, N_STATE)
    row = jax.ShapeDtypeStruct((1, N_STATE), F32)
    mat = jax.ShapeDtypeStruct((SSM_GROUP, N_STATE), F32)
    return pl.pallas_call(_discretize_kernel, out_shape=(row, row, mat, mat), name="s5_discretize")(
        lre, lim, dt, bre, bim)


def _block_diag_b(bbt_re, bbt_im):
    gph = SSM_GROUPS // SSM_HALVES
    eye = jnp.eye(gph, dtype=F32)

    def one(bbt):
        b = bbt.reshape(SSM_GROUP, SSM_HALVES, gph, SSM_STATE)
        m = jnp.einsum("chgp,gk->hgckp", b, eye)
        return m.reshape(SSM_HALVES, MXU_DIM, HALF_STATE)

    return jnp.concatenate([one(bbt_re), one(bbt_im)], axis=-1).astype(BF16)


def _block_diag_c(c_re, c_im):
    gph = SSM_GROUPS // SSM_HALVES
    eye = jnp.eye(gph, dtype=F32)

    def one(c):
        cc = c.astype(F32).reshape(SSM_HALVES, gph, SSM_GROUP, SSM_STATE)
        m = jnp.einsum("hgcp,gk->hgpkc", cc, eye)
        return m.reshape(SSM_HALVES, HALF_STATE, MXU_DIM)

    return jnp.concatenate([one(c_re), -one(c_im)], axis=1).astype(BF16)


def _ffn_kernel(x_ref, g_ref, w1_ref, w3_ref, w2_ref, gf_ref, o_ref, *, ff_chunk, to_time_major):
    xb = x_ref[...]
    if not to_time_major:
        xb = pltpu.einshape("tbd->btd", xb)
    nbg, tq, _ = xb.shape
    x = xb.reshape(nbg * tq, D_MODEL)
    h = _rms(x, g_ref[...]).astype(BF16)
    acc = jnp.zeros(x.shape, F32)
    for c in range(D_FF // ff_chunk):
        cols = slice(c * ff_chunk, (c + 1) * ff_chunk)
        a = _dot(h, w1_ref[:, cols])
        b = _dot(h, w3_ref[:, cols])
        gated = (a * _sigmoid(a) * b).astype(BF16)
        acc = acc + _dot(gated, w2_ref[cols, :])
    y = x + FFN_RES * acc
    if not to_time_major:
        y = _rms(y, gf_ref[...])
    y = y.reshape(nbg, tq, D_MODEL)
    if to_time_major:
        y = pltpu.einshape("btd->tbd", y)
    o_ref[...] = y


def _ffn(x, norm_g, w1, w3, w2, final_g, *, nb, length, tq, to_time_major):
    batch_major = pl.BlockSpec((SUBLANES, tq, D_MODEL), lambda g, i: (g, i, 0))
    time_major = pl.BlockSpec((tq, SUBLANES, D_MODEL), lambda g, i: (i, g, 0))
    if to_time_major:
        in_spec, out_spec = batch_major, time_major
        out_shape = jax.ShapeDtypeStruct((length, nb, D_MODEL), F32)
    else:
        in_spec, out_spec = time_major, batch_major
        out_shape = jax.ShapeDtypeStruct((nb, length, D_MODEL), F32)
    weight_bytes = 3 * D_MODEL * D_FF * 2
    block_bytes = SUBLANES * tq * D_MODEL * 4
    vmem = weight_bytes + 4 * block_bytes + 10 * block_bytes + (8 << 20)
    return pl.pallas_call(
        functools.partial(_ffn_kernel, ff_chunk=MXU_DIM, to_time_major=to_time_major),
        grid=(nb // SUBLANES, length // tq),
        in_specs=[in_spec, _const_spec((1, D_MODEL)), _const_spec((D_MODEL, D_FF)),
                  _const_spec((D_MODEL, D_FF)), _const_spec((D_FF, D_MODEL)), _const_spec((1, D_MODEL))],
        out_specs=out_spec,
        out_shape=out_shape,
        compiler_params=pltpu.CompilerParams(
            dimension_semantics=("arbitrary", "arbitrary"), vmem_limit_bytes=min(vmem, VMEM_BYTES - (6 << 20))),
        name="ffn_to_time_major" if to_time_major else "ffn_final",
    )(x, norm_g, w1, w3, w2, final_g)


def _merge(*streams):
    keyed = []
    for s, pieces in enumerate(streams):
        for i, piece in enumerate(pieces):
            keyed.append(((i + 0.5) / len(pieces), s, i, piece))
    return [piece for _, _, _, piece in sorted(keyed, key=lambda e: e[:3])]


def _mixer_kernel(x_ref, gn_ref, win_ref, are_ref, aim_ref, bm_ref, cm_ref, d_ref, gluw_ref, glub_ref, sow_ref,
                  cw_ref, cb_ref, lng_ref, lnb_ref, cow_ref, wo_ref, h0_ref, cache_ref,
                  o_ref, st_ref, cbuf_ref,
                  hb_scr, u_scr, bu_scr, vp_scr, conv_scr, cn_scr, gate_scr, y_scr, mix_scr, *, nb, q):
    step = pl.program_id(0)
    rows = nb * q
    hist_rows = HIST * nb
    groups = nb // SUBLANES

    @pl.when(step == 0)
    def _():
        st_ref[...] = h0_ref[...]
        vp_scr[0:hist_rows, :] = cache_ref[...]

    hb_scr[...] = _rms(x_ref[...], gn_ref[...]).astype(BF16)

    u_scr[...] = _dot(hb_scr[...], win_ref[:, 0:D_SSM])
    for h in range(SSM_HALVES):
        ub = u_scr[:, h * MXU_DIM:(h + 1) * MXU_DIM].astype(BF16)
        bu_scr[:, 2 * h * HALF_STATE:2 * (h + 1) * HALF_STATE] = _dot(ub, bm_ref[h])

    def glu_piece(c):
        def run():
            cols = slice(c * MXU_DIM, (c + 1) * MXU_DIM)
            conv_a = _dot(hb_scr[...], win_ref[:, D_SSM + c * MXU_DIM:D_SSM + (c + 1) * MXU_DIM])
            conv_g = _dot(hb_scr[...], win_ref[:, D_SSM + D_CONV + c * MXU_DIM:D_SSM + D_CONV + (c + 1) * MXU_DIM])
            vp_scr[hist_rows:hist_rows + rows, cols] = conv_a * _sigmoid(conv_g)
        return run

    carries = {}

    def scan_piece(h, j, t0):
        re_cols = slice(2 * h * HALF_STATE + j * SCAN_LANES, 2 * h * HALF_STATE + (j + 1) * SCAN_LANES)
        im_cols = slice(re_cols.start + HALF_STATE, re_cols.stop + HALF_STATE)
        a_cols = slice(h * HALF_STATE + j * SCAN_LANES, h * HALF_STATE + (j + 1) * SCAN_LANES)

        def run():
            ar = jnp.broadcast_to(are_ref[:, a_cols], (SUBLANES, SCAN_LANES))
            ai = jnp.broadcast_to(aim_ref[:, a_cols], (SUBLANES, SCAN_LANES))
            if t0 == 0:
                state = [(st_ref[bg * SUBLANES:(bg + 1) * SUBLANES, re_cols],
                          st_ref[bg * SUBLANES:(bg + 1) * SUBLANES, im_cols]) for bg in range(groups)]
            else:
                state = carries.pop((h, j))
            for t in range(t0, min(t0 + SCAN_T, q)):
                for bg in range(groups):
                    sr, si = state[bg]
                    r = slice(t * nb + bg * SUBLANES, t * nb + (bg + 1) * SUBLANES)
                    nr = ar * sr - ai * si + bu_scr[r, re_cols]
                    ni = ar * si + ai * sr + bu_scr[r, im_cols]
                    bu_scr[r, re_cols] = nr
                    bu_scr[r, im_cols] = ni
                    state[bg] = (nr, ni)
            if t0 + SCAN_T >= q:
                for bg in range(groups):
                    st_ref[bg * SUBLANES:(bg + 1) * SUBLANES, re_cols] = state[bg][0]
                    st_ref[bg * SUBLANES:(bg + 1) * SUBLANES, im_cols] = state[bg][1]
            else:
                carries[(h, j)] = state
        return run

    def y_piece(h):
        def run():
            s = bu_scr[:, 2 * h * HALF_STATE:2 * (h + 1) * HALF_STATE].astype(BF16)
            y_scr[:, h * MXU_DIM:(h + 1) * MXU_DIM] = _dot(s, cm_ref[h])
        return run

    def gate_piece(c):
        def run():
            cols = slice(c * MXU_DIM, (c + 1) * MXU_DIM)
            g = _dot(hb_scr[...], win_ref[:, GATE0 + c * MXU_DIM:GATE0 + (c + 1) * MXU_DIM])
            gate_scr[:, cols] = _sigmoid(g)
        return run

    def conv_piece(lc, rc):
        def run():
            lanes = slice(lc * LANES, (lc + 1) * LANES)
            r0 = rc * CONV_ROWS
            acc = jnp.broadcast_to(cb_ref[:, lanes], (CONV_ROWS, LANES))
            for k in range(CONV_WIDTH):
                acc = acc + cw_ref[k:k + 1, lanes] * vp_scr[r0 + k * nb:r0 + k * nb + CONV_ROWS, lanes]
            conv_scr[r0:r0 + CONV_ROWS, lanes] = acc
        return run

    def ln_piece(rc):
        def run():
            r = slice(rc * CONV_ROWS, (rc + 1) * CONV_ROWS)
            c = conv_scr[r, :]
            xc = c - jnp.mean(c, axis=-1, keepdims=True)
            var = jnp.mean(xc * xc, axis=-1, keepdims=True)
            ln = xc * lax.rsqrt(var + EPS) * lng_ref[...] + lnb_ref[...]
            cn_scr[r, :] = (ln * _sigmoid(ln)).astype(BF16)
        return run

    def ssm_glu_piece():
        def run():
            y = y_scr[...] + d_ref[...] * u_scr[...]
            z = 0.5 * y * (1.0 + jnp.tanh(math.sqrt(2.0 / math.pi) * (y + 0.044715 * (y * y * y))))
            y_scr[...] = z * _sigmoid(_dot(z.astype(BF16), gluw_ref[...]) + glub_ref[...])
        return run

    n_scan = -(-q // SCAN_T)
    scan = {(h, j): [scan_piece(h, j, i * SCAN_T) for i in range(n_scan)]
            for h in range(SSM_HALVES) for j in range(HALF_STATE // SCAN_LANES)}
    glu = [glu_piece(c) for c in range(D_CONV // MXU_DIM)]
    gates = [gate_piece(c) for c in range(2 * D_MODEL // MXU_DIM)]
    n_rc = rows // CONV_ROWS
    conv = [conv_piece(lc, rc) for lc in range(D_CONV // LANES) for rc in range(n_rc)]
    ln = [ln_piece(rc) for rc in range(n_rc)]

    for piece in _merge(scan[(0, 0)] + scan[(0, 1)], glu):
        piece()
    for piece in _merge(scan[(1, 0)] + scan[(1, 1)], [y_piece(0)] + gates[:2]):
        piece()
    for piece in _merge(conv + ln, [y_piece(1)] + gates[2:] + [ssm_glu_piece()]):
        piece()

    @pl.when(step == pl.num_programs(0) - 1)
    def _():
        cbuf_ref[...] = vp_scr[rows:rows + hist_rows, :]

    for r in range(0, hist_rows, nb):
        vp_scr[r:r + nb, :] = vp_scr[rows + r:rows + r + nb, :]

    zb = y_scr[...].astype(BF16)
    for c in range(D_MODEL // MXU_DIM):
        cols = slice(c * MXU_DIM, (c + 1) * MXU_DIM)
        br_s = _dot(zb, sow_ref[:, cols])
        br_c = _dot(cn_scr[...], cow_ref[:, cols])
        gate_c = gate_scr[:, D_MODEL + c * MXU_DIM:D_MODEL + (c + 1) * MXU_DIM]
        mix_scr[:, cols] = (gate_scr[:, cols] * br_s + gate_c * br_c).astype(BF16)
    o_ref[...] = x_ref[...] + _dot(mix_scr[...], wo_ref[...])


def _mixer(x_tm, p, h0, cache_tm, *, nb, length, q):
    rows = nb * q
    assert rows % CONV_ROWS == 0 and nb % SUBLANES == 0
    consts = [
        (p["mix_norm"], (1, D_MODEL)), (p["w_in"], (D_MODEL, N_IN)),
        (p["a_re"], (1, N_STATE)), (p["a_im"], (1, N_STATE)),
        (p["bm"], (SSM_HALVES, MXU_DIM, 2 * HALF_STATE)), (p["cm"], (SSM_HALVES, 2 * HALF_STATE, MXU_DIM)),
        (p["ssm_d"], (1, D_SSM)), (p["ssm_glu_w"], (D_SSM, D_SSM)), (p["ssm_glu_b"], (1, D_SSM)),
        (p["ssm_out_w"], (D_SSM, D_MODEL)),
        (p["conv_dw_w"], (CONV_WIDTH, D_CONV)), (p["conv_dw_b"], (1, D_CONV)),
        (p["conv_ln_g"], (1, D_CONV)), (p["conv_ln_b"], (1, D_CONV)), (p["conv_out_w"], (D_CONV, D_MODEL)),
        (p["w_o"], (D_MODEL, D_MODEL)),
        (h0, (nb, 2 * N_STATE)), (cache_tm, (HIST * nb, D_CONV)),
    ]
    x_spec = pl.BlockSpec((rows, D_MODEL), lambda i: (i, 0))
    scratch_shapes = [
        ((rows, D_MODEL), BF16),
        ((rows, D_SSM), F32),
        ((rows, 2 * N_STATE), F32),
        (((HIST + q) * nb, D_CONV), F32),
        ((rows, D_CONV), F32),
        ((rows, D_CONV), BF16),
        ((rows, 2 * D_MODEL), F32),
        ((rows, D_SSM), F32),
        ((rows, D_MODEL), BF16),
    ]
    const_bytes = sum(math.prod(s) * a.dtype.itemsize for a, s in consts)
    scratch_bytes = sum(math.prod(s) * jnp.dtype(d).itemsize for s, d in scratch_shapes)
    block_bytes = rows * D_MODEL * 4
    vmem = const_bytes + scratch_bytes + 4 * block_bytes + 6 * block_bytes + (4 << 20)
    return pl.pallas_call(
        functools.partial(_mixer_kernel, nb=nb, q=q),
        grid=(length // q,),
        in_specs=[x_spec] + [_const_spec(s) for _, s in consts],
        out_specs=(x_spec, pl.BlockSpec((nb, 2 * N_STATE), lambda i: (0, 0)),
                   pl.BlockSpec((HIST * nb, D_CONV), lambda i: (0, 0))),
        out_shape=(jax.ShapeDtypeStruct((length * nb, D_MODEL), F32),
                   jax.ShapeDtypeStruct((nb, 2 * N_STATE), F32),
                   jax.ShapeDtypeStruct((HIST * nb, D_CONV), F32)),
        scratch_shapes=[pltpu.VMEM(s, d) for s, d in scratch_shapes],
        compiler_params=pltpu.CompilerParams(
            dimension_semantics=("arbitrary",), vmem_limit_bytes=min(vmem, VMEM_BYTES - (6 << 20))),
        name="mixer",
    )(x_tm, *[a for a, _ in consts])


def _trunk(x, p, final_g, h0, cache_tm, *, tq, q):
    nb, length, _ = x.shape
    x1 = _ffn(x, p["ffn1_norm"], p["ffn1_w1"], p["ffn1_w3"], p["ffn1_w2"], final_g,
              nb=nb, length=length, tq=tq, to_time_major=True)
    x2, state, conv_tail = _mixer(x1.reshape(length * nb, D_MODEL), p, h0, cache_tm, nb=nb, length=length, q=q)
    y = _ffn(x2.reshape(length, nb, D_MODEL), p["ffn2_norm"], p["ffn2_w1"], p["ffn2_w3"], p["ffn2_w2"], final_g,
             nb=nb, length=length, tq=tq, to_time_major=False)
    st = state.reshape(nb, SSM_HALVES, 2, SSM_GROUPS // SSM_HALVES, SSM_STATE)
    new_re = st[:, :, 0].reshape(1, nb, SSM_GROUPS, SSM_STATE)
    new_im = st[:, :, 1].reshape(1, nb, SSM_GROUPS, SSM_STATE)
    new_conv = conv_tail.reshape(HIST, nb, D_CONV).transpose(1, 0, 2)[None]
    return y, new_re, new_im, new_conv


def _state_rows(re, im):
    nb = re.shape[0]
    gph = SSM_GROUPS // SSM_HALVES
    st = jnp.stack([re.reshape(nb, SSM_HALVES, gph, SSM_STATE), im.reshape(nb, SSM_HALVES, gph, SSM_STATE)], axis=2)
    return st.reshape(nb, 2 * N_STATE).astype(F32)


def kernel(x_prompt, x_sample, state_ssm_re, state_ssm_im, cache_conv, ffn1_norm, ffn1_w1, ffn1_w3, ffn1_w2, mix_norm, w_in, ssm_lambda_re, ssm_lambda_im, ssm_log_step, ssm_b_re, ssm_b_im, ssm_c_re, ssm_c_im, ssm_d, ssm_glu_w, ssm_glu_b, ssm_out_w, conv_dw_w, conv_dw_b, conv_ln_g, conv_ln_b, conv_out_w, w_o, ffn2_norm, ffn2_w1, ffn2_w3, ffn2_w2, final_norm):
    a_re, a_im, bbt_re, bbt_im = _discretize(ssm_lambda_re[0], ssm_lambda_im[0], ssm_log_step[0],
                                             ssm_b_re[0], ssm_b_im[0])
    row = lambda v: v.reshape(1, -1).astype(F32)
    p = dict(
        ffn1_norm=row(ffn1_norm[0]), ffn1_w1=ffn1_w1[0].astype(BF16), ffn1_w3=ffn1_w3[0].astype(BF16),
        ffn1_w2=ffn1_w2[0].astype(BF16),
        mix_norm=row(mix_norm[0]), w_in=w_in[0].astype(BF16),
        a_re=a_re, a_im=a_im, bm=_block_diag_b(bbt_re, bbt_im), cm=_block_diag_c(ssm_c_re[0], ssm_c_im[0]),
        ssm_d=row(ssm_d[0]), ssm_glu_w=ssm_glu_w[0].astype(BF16), ssm_glu_b=row(ssm_glu_b[0]),
        ssm_out_w=ssm_out_w[0].astype(BF16),
        conv_dw_w=conv_dw_w[0].astype(F32), conv_dw_b=row(conv_dw_b[0]),
        conv_ln_g=row(conv_ln_g[0]), conv_ln_b=row(conv_ln_b[0]), conv_out_w=conv_out_w[0].astype(BF16),
        w_o=w_o[0].astype(BF16),
        ffn2_norm=row(ffn2_norm[0]), ffn2_w1=ffn2_w1[0].astype(BF16), ffn2_w3=ffn2_w3[0].astype(BF16),
        ffn2_w2=ffn2_w2[0].astype(BF16),
    )
    final_g = row(final_norm)

    nb_p, len_p, _ = x_prompt.shape
    y_p, re_p, im_p, conv_p = _trunk(
        x_prompt, p, final_g, jnp.zeros((nb_p, 2 * N_STATE), F32), jnp.zeros((HIST * nb_p, D_CONV), F32),
        tq=64, q=16)

    nb_s, len_s, _ = x_sample.shape
    cache_tm = cache_conv[0].astype(F32).transpose(1, 0, 2).reshape(HIST * nb_s, D_CONV)
    y_s, re_s, im_s, conv_s = _trunk(
        x_sample, p, final_g, _state_rows(state_ssm_re[0], state_ssm_im[0]), cache_tm, tq=len_s, q=len_s)

    return (y_p, y_s, re_p, im_p, conv_p, re_s, im_s, conv_s)
```

```python
import functools
import math

import jax
import jax.numpy as jnp
from jax import lax
from jax.experimental import pallas as pl
from jax.experimental.pallas import tpu as pltpu

D_MODEL = 1024
D_SSM = 512
SSM_GROUP = 16
SSM_GROUPS = 32
SSM_STATE = 64
D_CONV = 512
CONV_WIDTH = 31
D_FF = 2816
FFN_RES = 0.5
EPS = 1e-6

SUBLANES = 8
LANES = 128
MXU_DIM = 256
VMEM_BYTES = 64 * 1024 * 1024

N_STATE = SSM_GROUPS * SSM_STATE
SSM_HALVES = D_SSM // MXU_DIM
HALF_STATE = N_STATE // SSM_HALVES
SCAN_LANES = 512
CONV_SLABS = D_CONV // LANES
HIST = CONV_WIDTH - 1
N_PROJ = D_SSM + 2 * D_CONV
N_GATE_CHUNKS = 2 * D_MODEL // MXU_DIM

F32 = jnp.float32
BF16 = jnp.bfloat16


def _sigmoid(x):
    return 1.0 / (1.0 + jnp.exp(-x))


def _rms(x, g):
    return x * lax.rsqrt(jnp.mean(x * x, axis=-1, keepdims=True) + EPS) * g


def _dot(a, b):
    return jnp.dot(a, b, preferred_element_type=F32)


def _const_spec(shape):
    return pl.BlockSpec(shape, lambda *_: (0,) * len(shape), pipeline_mode=pl.Buffered(1))


def _discretize_kernel(lre_ref, lim_ref, dt_ref, bre_ref, bim_ref, are_ref, aim_ref, bbre_ref, bbim_ref):
    lre = lre_ref[...]
    lim = lim_ref[...]
    dt = jnp.exp(dt_ref[...])
    mag = jnp.exp(lre * dt)
    a_re = mag * jnp.cos(lim * dt)
    a_im = mag * jnp.sin(lim * dt)
    num_re = a_re - 1.0
    inv_den = 1.0 / (lre * lre + lim * lim)
    k_re = (num_re * lre + a_im * lim) * inv_den
    k_im = (a_im * lre - num_re * lim) * inv_den
    are_ref[...] = a_re
    aim_ref[...] = a_im
    bre = bre_ref[...]
    bim = bim_ref[...]
    bbre_ref[...] = k_re * bre - k_im * bim
    bbim_ref[...] = k_re * bim + k_im * bre


def _discretize(lam_re, lam_im, log_step, b_re, b_im):
    lre = lam_re.reshape(1, N_STATE).astype(F32)
    lim = lam_im.reshape(1, N_STATE).astype(F32)
    dt = jnp.repeat(log_step.astype(F32), SSM_STATE).reshape(1, N_STATE)
    bre = b_re.astype(F32).transpose(2, 0, 1).reshape(SSM_GROUP, N_STATE)
    bim = b_im.astype(F32).transpose(2, 0, 1).reshape(SSM_GROUP, N_STATE)
    row = jax.ShapeDtypeStruct((1, N_STATE), F32)
    mat = jax.ShapeDtypeStruct((SSM_GROUP, N_STATE), F32)
    return pl.pallas_call(_discretize_kernel, out_shape=(row, row, mat, mat), name="s5_discretize")(
        lre, lim, dt, bre, bim)


def _block_diag_b(bbt_re, bbt_im):
    gph = SSM_GROUPS // SSM_HALVES
    eye = jnp.eye(gph, dtype=F32)

    def one(bbt):
        b = bbt.reshape(SSM_GROUP, SSM_HALVES, gph, SSM_STATE)
        m = jnp.einsum("chgp,gk->hgckp", b, eye)
        return m.reshape(SSM_HALVES, MXU_DIM, HALF_STATE)

    return jnp.concatenate([one(bbt_re), one(bbt_im)], axis=-1).astype(BF16)


def _block_diag_c(c_re, c_im):
    gph = SSM_GROUPS // SSM_HALVES
    eye = jnp.eye(gph, dtype=F32)

    def one(c):
        cc = c.astype(F32).reshape(SSM_HALVES, gph, SSM_GROUP, SSM_STATE)
        m = jnp.einsum("hgcp,gk->hgpkc", cc, eye)
        return m.reshape(SSM_HALVES, HALF_STATE, MXU_DIM)

    return jnp.concatenate([one(c_re), -one(c_im)], axis=1).astype(BF16)


def _ffn_kernel(x_ref, g_ref, w1_ref, w3_ref, w2_ref, gf_ref, o_ref, *, ff_chunk, to_time_major):
    xb = x_ref[...]
    if not to_time_major:
        xb = jnp.transpose(xb, (1, 0, 2))
    nbg, tq, _ = xb.shape
    x = xb.reshape(nbg * tq, D_MODEL)
    h = _rms(x, g_ref[...]).astype(BF16)
    acc = jnp.zeros(x.shape, F32)
    for c in range(D_FF // ff_chunk):
        cols = slice(c * ff_chunk, (c + 1) * ff_chunk)
        a = _dot(h, w1_ref[:, cols])
        b = _dot(h, w3_ref[:, cols])
        gated = (a * _sigmoid(a) * b).astype(BF16)
        acc = acc + _dot(gated, w2_ref[cols, :])
    y = x + FFN_RES * acc
    if not to_time_major:
        y = _rms(y, gf_ref[...])
    y = y.reshape(nbg, tq, D_MODEL)
    if to_time_major:
        y = jnp.transpose(y, (1, 0, 2))
    o_ref[...] = y


def _ffn(x, norm_g, w1, w3, w2, final_g, *, nb, length, tq, to_time_major):
    batch_major = pl.BlockSpec((SUBLANES, tq, D_MODEL), lambda g, i: (g, i, 0))
    time_major = pl.BlockSpec((tq, SUBLANES, D_MODEL), lambda g, i: (i, g, 0))
    if to_time_major:
        in_spec, out_spec = batch_major, time_major
        out_shape = jax.ShapeDtypeStruct((length, nb, D_MODEL), F32)
    else:
        in_spec, out_spec = time_major, batch_major
        out_shape = jax.ShapeDtypeStruct((nb, length, D_MODEL), F32)
    weight_bytes = 3 * D_MODEL * D_FF * 2
    block_bytes = SUBLANES * tq * D_MODEL * 4
    vmem = weight_bytes + 4 * block_bytes + 10 * block_bytes + (8 << 20)
    return pl.pallas_call(
        functools.partial(_ffn_kernel, ff_chunk=MXU_DIM, to_time_major=to_time_major),
        grid=(nb // SUBLANES, length // tq),
        in_specs=[in_spec, _const_spec((1, D_MODEL)), _const_spec((D_MODEL, D_FF)),
                  _const_spec((D_MODEL, D_FF)), _const_spec((D_FF, D_MODEL)), _const_spec((1, D_MODEL))],
        out_specs=out_spec,
        out_shape=out_shape,
        compiler_params=pltpu.CompilerParams(
            dimension_semantics=("arbitrary", "arbitrary"), vmem_limit_bytes=min(vmem, VMEM_BYTES - (6 << 20))),
        name="ffn_to_time_major" if to_time_major else "ffn_final",
    )(x, norm_g, w1, w3, w2, final_g)


def _mixer_kernel(x_ref, gn_ref, win_ref, wg_ref, are_ref, aim_ref, bm_ref, cm_ref, d_ref, gluw_ref, glub_ref,
                  sow_ref, cw_ref, cb_ref, lng_ref, lnb_ref, cow_ref, wo_ref, h0_ref, cache_ref,
                  o_ref, st_ref, cbuf_ref,
                  hb_scr, u_scr, bu_scr, vp_scr, conv_scr, gate_scr, y_scr, mix_scr, *, nb, q):
    step = pl.program_id(0)
    rows = nb * q
    hist_rows = HIST * nb
    groups = nb // SUBLANES
    n_slabs = CONV_SLABS

    in_grid = step < pl.num_programs(0)

    def stage(fn):
        pl.when(in_grid)(fn)

    def gate_piece(c):
        gate_scr[c] = _dot(hb_scr[...], wg_ref[c])

    def conv_slab(s):
        taps = [jnp.broadcast_to(cw_ref[s, k:k + 1, :], (SUBLANES, LANES)) for k in range(CONV_WIDTH)]
        bias = jnp.broadcast_to(cb_ref[s], (SUBLANES, LANES))
        for bg in range(groups):
            acc = [bias] * q
            for t_in in range(q + HIST):
                x_in = vp_scr[s, t_in * nb + bg * SUBLANES:t_in * nb + (bg + 1) * SUBLANES, :]
                for t in range(max(0, t_in - HIST), min(q, t_in + 1)):
                    acc[t] = acc[t] + taps[t_in - t] * x_in
            for t in range(q):
                conv_scr[s, t * nb + bg * SUBLANES:t * nb + (bg + 1) * SUBLANES, :] = acc[t]

    def bu_piece(h):
        ub = u_scr[:, h * MXU_DIM:(h + 1) * MXU_DIM].astype(BF16)
        bu_scr[:, 2 * h * HALF_STATE:2 * (h + 1) * HALF_STATE] = _dot(ub, bm_ref[h])

    def glu_piece(c):
        cols = slice(c * MXU_DIM, (c + 1) * MXU_DIM)
        conv_a = _dot(hb_scr[...], win_ref[:, D_SSM + c * MXU_DIM:D_SSM + (c + 1) * MXU_DIM])
        conv_g = _dot(hb_scr[...], win_ref[:, D_SSM + D_CONV + c * MXU_DIM:D_SSM + D_CONV + (c + 1) * MXU_DIM])
        v = conv_a * _sigmoid(conv_g)
        for s in range(MXU_DIM // LANES):
            vp_scr[c * (MXU_DIM // LANES) + s, hist_rows:hist_rows + rows, :] = v[:, s * LANES:(s + 1) * LANES]

    def y_piece(h, row_blocks):
        for r in [slice(i * rows // row_blocks, (i + 1) * rows // row_blocks) for i in range(row_blocks)]:
            s = bu_scr[r, 2 * h * HALF_STATE:2 * (h + 1) * HALF_STATE].astype(BF16)
            y_scr[r, h * MXU_DIM:(h + 1) * MXU_DIM] = _dot(s, cm_ref[h])

    def scan_pass(h, j):
        re_cols = slice(2 * h * HALF_STATE + j * SCAN_LANES, 2 * h * HALF_STATE + (j + 1) * SCAN_LANES)
        im_cols = slice(re_cols.start + HALF_STATE, re_cols.stop + HALF_STATE)
        a_cols = slice(h * HALF_STATE + j * SCAN_LANES, h * HALF_STATE + (j + 1) * SCAN_LANES)
        ar = jnp.broadcast_to(are_ref[:, a_cols], (SUBLANES, SCAN_LANES))
        ai = jnp.broadcast_to(aim_ref[:, a_cols], (SUBLANES, SCAN_LANES))
        state = [(st_ref[bg * SUBLANES:(bg + 1) * SUBLANES, re_cols],
                  st_ref[bg * SUBLANES:(bg + 1) * SUBLANES, im_cols]) for bg in range(groups)]
        for t in range(q):
            for bg in range(groups):
                sr, si = state[bg]
                r = slice(t * nb + bg * SUBLANES, t * nb + (bg + 1) * SUBLANES)
                nr = ar * sr - ai * si + bu_scr[r, re_cols]
                ni = ar * si + ai * sr + bu_scr[r, im_cols]
                bu_scr[r, re_cols] = nr
                bu_scr[r, im_cols] = ni
                state[bg] = (nr, ni)
        for bg in range(groups):
            st_ref[bg * SUBLANES:(bg + 1) * SUBLANES, re_cols] = state[bg][0]
            st_ref[bg * SUBLANES:(bg + 1) * SUBLANES, im_cols] = state[bg][1]

    @pl.when(step == 0)
    def _():
        st_ref[...] = h0_ref[...]
        for s in range(n_slabs):
            vp_scr[s, 0:hist_rows, :] = cache_ref[:, s * LANES:(s + 1) * LANES]

    hb_scr[...] = _rms(x_ref[...], gn_ref[...]).astype(BF16)
    u_scr[...] = _dot(hb_scr[...], win_ref[:, 0:D_SSM])
    bu_piece(0)

    @stage
    def _():
        scan_pass(0, 0)
        bu_piece(1)

    @stage
    def _():
        scan_pass(0, 1)
        glu_piece(0)

    @stage
    def _():
        scan_pass(1, 0)
        glu_piece(1)

    @stage
    def _():
        scan_pass(1, 1)
        y_piece(0, 1)

    gates_per_slab = N_GATE_CHUNKS // n_slabs

    def conv_body(s, carry):
        for g in range(gates_per_slab):
            gate_piece(s * gates_per_slab + g)
        conv_slab(s)
        return carry

    lax.fori_loop(0, n_slabs, conv_body, 0)

    @pl.when(step == pl.num_programs(0) - 1)
    def _():
        for s in range(n_slabs):
            cbuf_ref[:, s * LANES:(s + 1) * LANES] = vp_scr[s, rows:rows + hist_rows, :]

    for s in range(n_slabs):
        for r in range(0, hist_rows, nb):
            vp_scr[s, r:r + nb, :] = vp_scr[s, rows + r:rows + r + nb, :]

    y_piece(1, 2)
    c = jnp.concatenate([conv_scr[s] for s in range(n_slabs)], axis=-1)
    xc = c - jnp.mean(c, axis=-1, keepdims=True)
    var = jnp.mean(xc * xc, axis=-1, keepdims=True)
    ln = xc * lax.rsqrt(var + EPS) * lng_ref[...] + lnb_ref[...]
    cn = (ln * _sigmoid(ln)).astype(BF16)

    y = y_scr[...] + d_ref[...] * u_scr[...]
    z = 0.5 * y * (1.0 + jnp.tanh(math.sqrt(2.0 / math.pi) * (y + 0.044715 * (y * y * y))))
    zb = (z * _sigmoid(_dot(z.astype(BF16), gluw_ref[...]) + glub_ref[...])).astype(BF16)

    for c in range(D_MODEL // MXU_DIM):
        cols = slice(c * MXU_DIM, (c + 1) * MXU_DIM)
        br_s = _dot(zb, sow_ref[:, cols])
        br_c = _dot(cn, cow_ref[:, cols])
        gate_s = _sigmoid(gate_scr[c])
        gate_c = _sigmoid(gate_scr[D_MODEL // MXU_DIM + c])
        mix_scr[:, cols] = (gate_s * br_s + gate_c * br_c).astype(BF16)
    o_ref[...] = x_ref[...] + _dot(mix_scr[...], wo_ref[...])


def _mixer(x_tm, p, h0, cache_tm, *, nb, length, q):
    rows = nb * q
    assert nb % SUBLANES == 0 and N_GATE_CHUNKS % CONV_SLABS == 0
    consts = [
        (p["mix_norm"], (1, D_MODEL)), (p["w_proj"], (D_MODEL, N_PROJ)), (p["w_gate"], (N_GATE_CHUNKS, D_MODEL, MXU_DIM)),
        (p["a_re"], (1, N_STATE)), (p["a_im"], (1, N_STATE)),
        (p["bm"], (SSM_HALVES, MXU_DIM, 2 * HALF_STATE)), (p["cm"], (SSM_HALVES, 2 * HALF_STATE, MXU_DIM)),
        (p["ssm_d"], (1, D_SSM)), (p["ssm_glu_w"], (D_SSM, D_SSM)), (p["ssm_glu_b"], (1, D_SSM)),
        (p["ssm_out_w"], (D_SSM, D_MODEL)),
        (p["conv_dw_w"], (CONV_SLABS, CONV_WIDTH, LANES)), (p["conv_dw_b"], (CONV_SLABS, 1, LANES)),
        (p["conv_ln_g"], (1, D_CONV)), (p["conv_ln_b"], (1, D_CONV)), (p["conv_out_w"], (D_CONV, D_MODEL)),
        (p["w_o"], (D_MODEL, D_MODEL)),
        (h0, (nb, 2 * N_STATE)), (cache_tm, (HIST * nb, D_CONV)),
    ]
    x_spec = pl.BlockSpec((rows, D_MODEL), lambda i: (i, 0))
    scratch_shapes = [
        ((rows, D_MODEL), BF16),
        ((rows, D_SSM), F32),
        ((rows, 2 * N_STATE), F32),
        ((CONV_SLABS, (HIST + q) * nb, LANES), F32),
        ((CONV_SLABS, rows, LANES), F32),
        ((N_GATE_CHUNKS, rows, MXU_DIM), F32),
        ((rows, D_SSM), F32),
        ((rows, D_MODEL), BF16),
    ]
    const_bytes = sum(math.prod(s) * a.dtype.itemsize for a, s in consts)
    scratch_bytes = sum(math.prod(s) * jnp.dtype(d).itemsize for s, d in scratch_shapes)
    block_bytes = rows * D_MODEL * 4
    vmem = const_bytes + scratch_bytes + 4 * block_bytes + 6 * block_bytes + (4 << 20)
    return pl.pallas_call(
        functools.partial(_mixer_kernel, nb=nb, q=q),
        grid=(length // q,),
        in_specs=[x_spec] + [_const_spec(s) for _, s in consts],
        out_specs=(x_spec, pl.BlockSpec((nb, 2 * N_STATE), lambda i: (0, 0)),
                   pl.BlockSpec((HIST * nb, D_CONV), lambda i: (0, 0))),
        out_shape=(jax.ShapeDtypeStruct((length * nb, D_MODEL), F32),
                   jax.ShapeDtypeStruct((nb, 2 * N_STATE), F32),
                   jax.ShapeDtypeStruct((HIST * nb, D_CONV), F32)),
        scratch_shapes=[pltpu.VMEM(s, d) for s, d in scratch_shapes],
        compiler_params=pltpu.CompilerParams(
            dimension_semantics=("arbitrary",), vmem_limit_bytes=min(vmem, VMEM_BYTES - (6 << 20))),
        name="mixer",
    )(x_tm, *[a for a, _ in consts])


def _trunk(x, p, final_g, h0, cache_tm, *, tq, q):
    nb, length, _ = x.shape
    x1 = _ffn(x, p["ffn1_norm"], p["ffn1_w1"], p["ffn1_w3"], p["ffn1_w2"], final_g,
              nb=nb, length=length, tq=tq, to_time_major=True)
    x2, state, conv_tail = _mixer(x1.reshape(length * nb, D_MODEL), p, h0, cache_tm, nb=nb, length=length, q=q)
    y = _ffn(x2.reshape(length, nb, D_MODEL), p["ffn2_norm"], p["ffn2_w1"], p["ffn2_w3"], p["ffn2_w2"], final_g,
             nb=nb, length=length, tq=tq, to_time_major=False)
    st = state.reshape(nb, SSM_HALVES, 2, SSM_GROUPS // SSM_HALVES, SSM_STATE)
    new_re = st[:, :, 0].reshape(1, nb, SSM_GROUPS, SSM_STATE)
    new_im = st[:, :, 1].reshape(1, nb, SSM_GROUPS, SSM_STATE)
    new_conv = conv_tail.reshape(HIST, nb, D_CONV).transpose(1, 0, 2)[None]
    return y, new_re, new_im, new_conv


def _state_rows(re, im):
    nb = re.shape[0]
    gph = SSM_GROUPS // SSM_HALVES
    st = jnp.stack([re.reshape(nb, SSM_HALVES, gph, SSM_STATE), im.reshape(nb, SSM_HALVES, gph, SSM_STATE)], axis=2)
    return st.reshape(nb, 2 * N_STATE).astype(F32)


def kernel(x_prompt, x_sample, state_ssm_re, state_ssm_im, cache_conv, ffn1_norm, ffn1_w1, ffn1_w3, ffn1_w2, mix_norm, w_in, ssm_lambda_re, ssm_lambda_im, ssm_log_step, ssm_b_re, ssm_b_im, ssm_c_re, ssm_c_im, ssm_d, ssm_glu_w, ssm_glu_b, ssm_out_w, conv_dw_w, conv_dw_b, conv_ln_g, conv_ln_b, conv_out_w, w_o, ffn2_norm, ffn2_w1, ffn2_w3, ffn2_w2, final_norm):
    a_re, a_im, bbt_re, bbt_im = _discretize(ssm_lambda_re[0], ssm_lambda_im[0], ssm_log_step[0],
                                             ssm_b_re[0], ssm_b_im[0])
    row = lambda v: v.reshape(1, -1).astype(F32)
    w_in_b = w_in[0].astype(BF16)
    w_gate = w_in_b[:, N_PROJ:].reshape(D_MODEL, N_GATE_CHUNKS, MXU_DIM).transpose(1, 0, 2)
    p = dict(
        ffn1_norm=row(ffn1_norm[0]), ffn1_w1=ffn1_w1[0].astype(BF16), ffn1_w3=ffn1_w3[0].astype(BF16),
        ffn1_w2=ffn1_w2[0].astype(BF16),
        mix_norm=row(mix_norm[0]), w_proj=w_in_b[:, :N_PROJ], w_gate=w_gate,
        a_re=a_re, a_im=a_im, bm=_block_diag_b(bbt_re, bbt_im), cm=_block_diag_c(ssm_c_re[0], ssm_c_im[0]),
        ssm_d=row(ssm_d[0]), ssm_glu_w=ssm_glu_w[0].astype(BF16), ssm_glu_b=row(ssm_glu_b[0]),
        ssm_out_w=ssm_out_w[0].astype(BF16),
        conv_dw_w=conv_dw_w[0].astype(F32).reshape(CONV_WIDTH, CONV_SLABS, LANES).transpose(1, 0, 2),
        conv_dw_b=conv_dw_b[0].astype(F32).reshape(CONV_SLABS, 1, LANES),
        conv_ln_g=row(conv_ln_g[0]), conv_ln_b=row(conv_ln_b[0]), conv_out_w=conv_out_w[0].astype(BF16),
        w_o=w_o[0].astype(BF16),
        ffn2_norm=row(ffn2_norm[0]), ffn2_w1=ffn2_w1[0].astype(BF16), ffn2_w3=ffn2_w3[0].astype(BF16),
        ffn2_w2=ffn2_w2[0].astype(BF16),
    )
    final_g = row(final_norm)

    nb_p, len_p, _ = x_prompt.shape
    y_p, re_p, im_p, conv_p = _trunk(
        x_prompt, p, final_g, jnp.zeros((nb_p, 2 * N_STATE), F32), jnp.zeros((HIST * nb_p, D_CONV), F32),
        tq=64, q=16)

    nb_s, len_s, _ = x_sample.shape
    cache_tm = cache_conv[0].astype(F32).transpose(1, 0, 2).reshape(HIST * nb_s, D_CONV)
    y_s, re_s, im_s, conv_s = _trunk(
        x_sample, p, final_g, _state_rows(state_ssm_re[0], state_ssm_im[0]), cache_tm, tq=len_s, q=len_s)

    return (y_p, y_s, re_p, im_p, conv_p, re_s, im_s, conv_s)
```

```python
import functools
import math

import jax
import jax.numpy as jnp
from jax import lax
from jax.experimental import pallas as pl
from jax.experimental.pallas import tpu as pltpu

D_MODEL = 1024
D_SSM = 512
SSM_GROUP = 16
SSM_GROUPS = 32
SSM_STATE = 64
D_CONV = 512
CONV_WIDTH = 31
D_FF = 2816
FFN_RES = 0.5
EPS = 1e-6

SUBLANES = 8
LANES = 128
MXU_DIM = 256
VMEM_BYTES = 64 * 1024 * 1024

N_STATE = SSM_GROUPS * SSM_STATE
SSM_HALVES = D_SSM // MXU_DIM
HALF_STATE = N_STATE // SSM_HALVES
SCAN_LANES = 512
CONV_SLABS = D_CONV // LANES
HIST = CONV_WIDTH - 1
N_PROJ = D_SSM + 2 * D_CONV
N_GATE_CHUNKS = 2 * D_MODEL // MXU_DIM

F32 = jnp.float32
BF16 = jnp.bfloat16


def _sigmoid(x):
    return 1.0 / (1.0 + jnp.exp(-x))


def _rms(x, g):
    return x * lax.rsqrt(jnp.mean(x * x, axis=-1, keepdims=True) + EPS) * g


def _dot(a, b):
    return jnp.dot(a, b, preferred_element_type=F32)


def _const_spec(shape):
    return pl.BlockSpec(shape, lambda *_: (0,) * len(shape), pipeline_mode=pl.Buffered(1))


def _discretize_kernel(lre_ref, lim_ref, dt_ref, bre_ref, bim_ref, are_ref, aim_ref, bbre_ref, bbim_ref):
    lre = lre_ref[...]
    lim = lim_ref[...]
    dt = jnp.exp(dt_ref[...])
    mag = jnp.exp(lre * dt)
    a_re = mag * jnp.cos(lim * dt)
    a_im = mag * jnp.sin(lim * dt)
    num_re = a_re - 1.0
    inv_den = 1.0 / (lre * lre + lim * lim)
    k_re = (num_re * lre + a_im * lim) * inv_den
    k_im = (a_im * lre - num_re * lim) * inv_den
    are_ref[...] = a_re
    aim_ref[...] = a_im
    bre = bre_ref[...]
    bim = bim_ref[...]
    bbre_ref[...] = k_re * bre - k_im * bim
    bbim_ref[...] = k_re * bim + k_im * bre


def _discretize(lam_re, lam_im, log_step, b_re, b_im):
    lre = lam_re.reshape(1, N_STATE).astype(F32)
    lim = lam_im.reshape(1, N_STATE).astype(F32)
    dt = jnp.repeat(log_step.astype(F32), SSM_STATE).reshape(1, N_STATE)
    bre = b_re.astype(F32).transpose(2, 0, 1).reshape(SSM_GROUP, N_STATE)
    bim = b_im.astype(F32).transpose(2, 0, 1).reshape(SSM_GROUP, N_STATE)
    row = jax.ShapeDtypeStruct((1, N_STATE), F32)
    mat = jax.ShapeDtypeStruct((SSM_GROUP, N_STATE), F32)
    return pl.pallas_call(_discretize_kernel, out_shape=(row, row, mat, mat), name="s5_discretize")(
        lre, lim, dt, bre, bim)


def _block_diag_b(bbt_re, bbt_im):
    gph = SSM_GROUPS // SSM_HALVES
    eye = jnp.eye(gph, dtype=F32)

    def one(bbt):
        b = bbt.reshape(SSM_GROUP, SSM_HALVES, gph, SSM_STATE)
        m = jnp.einsum("chgp,gk->hgckp", b, eye)
        return m.reshape(SSM_HALVES, MXU_DIM, HALF_STATE)

    return jnp.concatenate([one(bbt_re), one(bbt_im)], axis=-1).astype(BF16)


def _block_diag_c(c_re, c_im):
    gph = SSM_GROUPS // SSM_HALVES
    eye = jnp.eye(gph, dtype=F32)

    def one(c):
        cc = c.astype(F32).reshape(SSM_HALVES, gph, SSM_GROUP, SSM_STATE)
        m = jnp.einsum("hgcp,gk->hgpkc", cc, eye)
        return m.reshape(SSM_HALVES, HALF_STATE, MXU_DIM)

    return jnp.concatenate([one(c_re), -one(c_im)], axis=1).astype(BF16)


def _ffn_kernel(x_ref, g_ref, w1_ref, w3_ref, w2_ref, gf_ref, o_ref, *, ff_chunk, to_time_major):
    xb = x_ref[...]
    if not to_time_major:
        xb = jnp.transpose(xb, (1, 0, 2))
    nbg, tq, _ = xb.shape
    x = xb.reshape(nbg * tq, D_MODEL)
    h = _rms(x, g_ref[...]).astype(BF16)
    acc = jnp.zeros(x.shape, F32)
    for c in range(D_FF // ff_chunk):
        cols = slice(c * ff_chunk, (c + 1) * ff_chunk)
        a = _dot(h, w1_ref[:, cols])
        b = _dot(h, w3_ref[:, cols])
        gated = (a * _sigmoid(a) * b).astype(BF16)
        acc = acc + _dot(gated, w2_ref[cols, :])
    y = x + acc
    if not to_time_major:
        y = _rms(y, gf_ref[...])
    y = y.reshape(nbg, tq, D_MODEL)
    if to_time_major:
        y = jnp.transpose(y, (1, 0, 2))
    o_ref[...] = y


def _ffn(x, norm_g, w1, w3, w2, final_g, *, nb, length, tq, to_time_major):
    batch_major = pl.BlockSpec((SUBLANES, tq, D_MODEL), lambda g, i: (g, i, 0))
    time_major = pl.BlockSpec((tq, SUBLANES, D_MODEL), lambda g, i: (i, g, 0))
    if to_time_major:
        in_spec, out_spec = batch_major, time_major
        out_shape = jax.ShapeDtypeStruct((length, nb, D_MODEL), F32)
    else:
        in_spec, out_spec = time_major, batch_major
        out_shape = jax.ShapeDtypeStruct((nb, length, D_MODEL), F32)
    weight_bytes = 3 * D_MODEL * D_FF * 2
    block_bytes = SUBLANES * tq * D_MODEL * 4
    vmem = weight_bytes + 4 * block_bytes + 10 * block_bytes + (8 << 20)
    return pl.pallas_call(
        functools.partial(_ffn_kernel, ff_chunk=MXU_DIM, to_time_major=to_time_major),
        grid=(nb // SUBLANES, length // tq),
        in_specs=[in_spec, _const_spec((1, D_MODEL)), _const_spec((D_MODEL, D_FF)),
                  _const_spec((D_MODEL, D_FF)), _const_spec((D_FF, D_MODEL)), _const_spec((1, D_MODEL))],
        out_specs=out_spec,
        out_shape=out_shape,
        compiler_params=pltpu.CompilerParams(
            dimension_semantics=("arbitrary", "arbitrary"), vmem_limit_bytes=min(vmem, VMEM_BYTES - (6 << 20))),
        name="ffn_to_time_major" if to_time_major else "ffn_final",
    )(x, norm_g, w1, w3, w2, final_g)


def _mixer_kernel(x_ref, gn_ref, win_ref, wg_ref, are_ref, aim_ref, bm_ref, cm_ref, d_ref, gluw_ref, glub_ref,
                  sow_ref, cw_ref, cb_ref, lng_ref, lnb_ref, cow_ref, wo_ref, h0_ref, cache_ref,
                  o_ref, st_ref, cbuf_ref,
                  hb_scr, u_scr, bu_scr, vp_scr, conv_scr, gate_scr, y_scr, mix_scr, *, nb, q):
    step = pl.program_id(0)
    rows = nb * q
    hist_rows = HIST * nb
    groups = nb // SUBLANES
    n_slabs = CONV_SLABS

    in_grid = step < pl.num_programs(0)

    def stage(fn):
        pl.when(in_grid)(fn)

    def gate_piece(c):
        gate_scr[c] = _dot(hb_scr[...], wg_ref[c])

    def conv_slab(s):
        taps = [jnp.broadcast_to(cw_ref[s, k:k + 1, :], (SUBLANES, LANES)) for k in range(CONV_WIDTH)]
        bias = jnp.broadcast_to(cb_ref[s], (SUBLANES, LANES))
        for bg in range(groups):
            acc = [bias] * q
            for t_in in range(q + HIST):
                x_in = vp_scr[s, t_in * nb + bg * SUBLANES:t_in * nb + (bg + 1) * SUBLANES, :]
                for t in range(max(0, t_in - HIST), min(q, t_in + 1)):
                    acc[t] = acc[t] + taps[t_in - t] * x_in
            for t in range(q):
                conv_scr[s, t * nb + bg * SUBLANES:t * nb + (bg + 1) * SUBLANES, :] = acc[t]

    def bu_piece(h):
        ub = u_scr[:, h * MXU_DIM:(h + 1) * MXU_DIM].astype(BF16)
        bu_scr[:, 2 * h * HALF_STATE:2 * (h + 1) * HALF_STATE] = _dot(ub, bm_ref[h])

    def glu_piece(c):
        cols = slice(c * MXU_DIM, (c + 1) * MXU_DIM)
        conv_a = _dot(hb_scr[...], win_ref[:, D_SSM + c * MXU_DIM:D_SSM + (c + 1) * MXU_DIM])
        conv_g = _dot(hb_scr[...], win_ref[:, D_SSM + D_CONV + c * MXU_DIM:D_SSM + D_CONV + (c + 1) * MXU_DIM])
        v = conv_a * _sigmoid(conv_g)
        for s in range(MXU_DIM // LANES):
            vp_scr[c * (MXU_DIM // LANES) + s, hist_rows:hist_rows + rows, :] = v[:, s * LANES:(s + 1) * LANES]

    def y_piece(h, row_blocks):
        for r in [slice(i * rows // row_blocks, (i + 1) * rows // row_blocks) for i in range(row_blocks)]:
            s = bu_scr[r, 2 * h * HALF_STATE:2 * (h + 1) * HALF_STATE].astype(BF16)
            y_scr[r, h * MXU_DIM:(h + 1) * MXU_DIM] = _dot(s, cm_ref[h])

    def scan_pass(h, j):
        re_cols = slice(2 * h * HALF_STATE + j * SCAN_LANES, 2 * h * HALF_STATE + (j + 1) * SCAN_LANES)
        im_cols = slice(re_cols.start + HALF_STATE, re_cols.stop + HALF_STATE)
        a_cols = slice(h * HALF_STATE + j * SCAN_LANES, h * HALF_STATE + (j + 1) * SCAN_LANES)
        ar = jnp.broadcast_to(are_ref[:, a_cols], (SUBLANES, SCAN_LANES))
        ai = jnp.broadcast_to(aim_ref[:, a_cols], (SUBLANES, SCAN_LANES))
        state = [(st_ref[bg * SUBLANES:(bg + 1) * SUBLANES, re_cols],
                  st_ref[bg * SUBLANES:(bg + 1) * SUBLANES, im_cols]) for bg in range(groups)]
        for t in range(q):
            for bg in range(groups):
                sr, si = state[bg]
                r = slice(t * nb + bg * SUBLANES, t * nb + (bg + 1) * SUBLANES)
                nr = ar * sr - ai * si + bu_scr[r, re_cols]
                ni = ar * si + ai * sr + bu_scr[r, im_cols]
                bu_scr[r, re_cols] = nr
                bu_scr[r, im_cols] = ni
                state[bg] = (nr, ni)
        for bg in range(groups):
            st_ref[bg * SUBLANES:(bg + 1) * SUBLANES, re_cols] = state[bg][0]
            st_ref[bg * SUBLANES:(bg + 1) * SUBLANES, im_cols] = state[bg][1]

    @pl.when(step == 0)
    def _():
        st_ref[...] = h0_ref[...]
        for s in range(n_slabs):
            vp_scr[s, 0:hist_rows, :] = cache_ref[:, s * LANES:(s + 1) * LANES]

    hb_scr[...] = _rms(x_ref[...], gn_ref[...]).astype(BF16)
    u_scr[...] = _dot(hb_scr[...], win_ref[:, 0:D_SSM])
    bu_piece(0)

    @stage
    def _():
        scan_pass(0, 0)
        bu_piece(1)

    @stage
    def _():
        scan_pass(0, 1)
        glu_piece(0)

    @stage
    def _():
        scan_pass(1, 0)
        glu_piece(1)

    @stage
    def _():
        scan_pass(1, 1)
        y_piece(0, 1)

    gates_per_slab = N_GATE_CHUNKS // n_slabs

    def conv_body(s, carry):
        for g in range(gates_per_slab):
            gate_piece(s * gates_per_slab + g)
        conv_slab(s)
        return carry

    lax.fori_loop(0, n_slabs, conv_body, 0)

    @pl.when(step == pl.num_programs(0) - 1)
    def _():
        for s in range(n_slabs):
            cbuf_ref[:, s * LANES:(s + 1) * LANES] = vp_scr[s, rows:rows + hist_rows, :]

    for s in range(n_slabs):
        for r in range(0, hist_rows, nb):
            vp_scr[s, r:r + nb, :] = vp_scr[s, rows + r:rows + r + nb, :]

    y_piece(1, 2)
    c = jnp.concatenate([conv_scr[s] for s in range(n_slabs)], axis=-1)
    xc = c - jnp.mean(c, axis=-1, keepdims=True)
    var = jnp.mean(xc * xc, axis=-1, keepdims=True)
    ln = xc * lax.rsqrt(var + EPS) * lng_ref[...] + lnb_ref[...]
    cn = (ln * _sigmoid(ln)).astype(BF16)

    y = y_scr[...] + d_ref[...] * u_scr[...]
    z = 0.5 * y * (1.0 + jnp.tanh(math.sqrt(2.0 / math.pi) * (y + 0.044715 * (y * y * y))))
    zb = (z * _sigmoid(_dot(z.astype(BF16), gluw_ref[...]) + glub_ref[...])).astype(BF16)

    for c in range(D_MODEL // MXU_DIM):
        cols = slice(c * MXU_DIM, (c + 1) * MXU_DIM)
        br_s = _dot(zb, sow_ref[:, cols])
        br_c = _dot(cn, cow_ref[:, cols])
        gate_s = _sigmoid(gate_scr[c])
        gate_c = _sigmoid(gate_scr[D_MODEL // MXU_DIM + c])
        mix_scr[:, cols] = (gate_s * br_s + gate_c * br_c).astype(BF16)
    o_ref[...] = x_ref[...] + _dot(mix_scr[...], wo_ref[...])


def _mixer(x_tm, p, h0, cache_tm, *, nb, length, q):
    rows = nb * q
    assert nb % SUBLANES == 0 and N_GATE_CHUNKS % CONV_SLABS == 0
    consts = [
        (p["mix_norm"], (1, D_MODEL)), (p["w_proj"], (D_MODEL, N_PROJ)), (p["w_gate"], (N_GATE_CHUNKS, D_MODEL, MXU_DIM)),
        (p["a_re"], (1, N_STATE)), (p["a_im"], (1, N_STATE)),
        (p["bm"], (SSM_HALVES, MXU_DIM, 2 * HALF_STATE)), (p["cm"], (SSM_HALVES, 2 * HALF_STATE, MXU_DIM)),
        (p["ssm_d"], (1, D_SSM)), (p["ssm_glu_w"], (D_SSM, D_SSM)), (p["ssm_glu_b"], (1, D_SSM)),
        (p["ssm_out_w"], (D_SSM, D_MODEL)),
        (p["conv_dw_w"], (CONV_SLABS, CONV_WIDTH, LANES)), (p["conv_dw_b"], (CONV_SLABS, 1, LANES)),
        (p["conv_ln_g"], (1, D_CONV)), (p["conv_ln_b"], (1, D_CONV)), (p["conv_out_w"], (D_CONV, D_MODEL)),
        (p["w_o"], (D_MODEL, D_MODEL)),
        (h0, (nb, 2 * N_STATE)), (cache_tm, (HIST * nb, D_CONV)),
    ]
    x_spec = pl.BlockSpec((rows, D_MODEL), lambda i: (i, 0))
    scratch_shapes = [
        ((rows, D_MODEL), BF16),
        ((rows, D_SSM), F32),
        ((rows, 2 * N_STATE), F32),
        ((CONV_SLABS, (HIST + q) * nb, LANES), F32),
        ((CONV_SLABS, rows, LANES), F32),
        ((N_GATE_CHUNKS, rows, MXU_DIM), F32),
        ((rows, D_SSM), F32),
        ((rows, D_MODEL), BF16),
    ]
    const_bytes = sum(math.prod(s) * a.dtype.itemsize for a, s in consts)
    scratch_bytes = sum(math.prod(s) * jnp.dtype(d).itemsize for s, d in scratch_shapes)
    block_bytes = rows * D_MODEL * 4
    vmem = const_bytes + scratch_bytes + 4 * block_bytes + 6 * block_bytes + (4 << 20)
    return pl.pallas_call(
        functools.partial(_mixer_kernel, nb=nb, q=q),
        grid=(length // q,),
        in_specs=[x_spec] + [_const_spec(s) for _, s in consts],
        out_specs=(x_spec, pl.BlockSpec((nb, 2 * N_STATE), lambda i: (0, 0)),
                   pl.BlockSpec((HIST * nb, D_CONV), lambda i: (0, 0))),
        out_shape=(jax.ShapeDtypeStruct((length * nb, D_MODEL), F32),
                   jax.ShapeDtypeStruct((nb, 2 * N_STATE), F32),
                   jax.ShapeDtypeStruct((HIST * nb, D_CONV), F32)),
        scratch_shapes=[pltpu.VMEM(s, d) for s, d in scratch_shapes],
        compiler_params=pltpu.CompilerParams(
            dimension_semantics=("arbitrary",), vmem_limit_bytes=min(vmem, VMEM_BYTES - (6 << 20))),
        name="mixer",
    )(x_tm, *[a for a, _ in consts])


def _trunk(x, p, final_g, h0, cache_tm, *, tq, q):
    nb, length, _ = x.shape
    x1 = _ffn(x, p["ffn1_norm"], p["ffn1_w1"], p["ffn1_w3"], p["ffn1_w2"], final_g,
              nb=nb, length=length, tq=tq, to_time_major=True)
    x2, state, conv_tail = _mixer(x1.reshape(length * nb, D_MODEL), p, h0, cache_tm, nb=nb, length=length, q=q)
    y = _ffn(x2.reshape(length, nb, D_MODEL), p["ffn2_norm"], p["ffn2_w1"], p["ffn2_w3"], p["ffn2_w2"], final_g,
             nb=nb, length=length, tq=tq, to_time_major=False)
    st = state.reshape(nb, SSM_HALVES, 2, SSM_GROUPS // SSM_HALVES, SSM_STATE)
    new_re = st[:, :, 0].reshape(1, nb, SSM_GROUPS, SSM_STATE)
    new_im = st[:, :, 1].reshape(1, nb, SSM_GROUPS, SSM_STATE)
    new_conv = conv_tail.reshape(HIST, nb, D_CONV).transpose(1, 0, 2)[None]
    return y, new_re, new_im, new_conv


def _state_rows(re, im):
    nb = re.shape[0]
    gph = SSM_GROUPS // SSM_HALVES
    st = jnp.stack([re.reshape(nb, SSM_HALVES, gph, SSM_STATE), im.reshape(nb, SSM_HALVES, gph, SSM_STATE)], axis=2)
    return st.reshape(nb, 2 * N_STATE).astype(F32)


def kernel(x_prompt, x_sample, state_ssm_re, state_ssm_im, cache_conv, ffn1_norm, ffn1_w1, ffn1_w3, ffn1_w2, mix_norm, w_in, ssm_lambda_re, ssm_lambda_im, ssm_log_step, ssm_b_re, ssm_b_im, ssm_c_re, ssm_c_im, ssm_d, ssm_glu_w, ssm_glu_b, ssm_out_w, conv_dw_w, conv_dw_b, conv_ln_g, conv_ln_b, conv_out_w, w_o, ffn2_norm, ffn2_w1, ffn2_w3, ffn2_w2, final_norm):
    a_re, a_im, bbt_re, bbt_im = _discretize(ssm_lambda_re[0], ssm_lambda_im[0], ssm_log_step[0],
                                             ssm_b_re[0], ssm_b_im[0])
    row = lambda v: v.reshape(1, -1).astype(F32)
    w_in_b = w_in[0].astype(BF16)
    w_gate = w_in_b[:, N_PROJ:].reshape(D_MODEL, N_GATE_CHUNKS, MXU_DIM).transpose(1, 0, 2)
    p = dict(
        ffn1_norm=row(ffn1_norm[0]), ffn1_w1=ffn1_w1[0].astype(BF16), ffn1_w3=ffn1_w3[0].astype(BF16),
        ffn1_w2=(FFN_RES * ffn1_w2[0]).astype(BF16),
        mix_norm=row(mix_norm[0]), w_proj=w_in_b[:, :N_PROJ], w_gate=w_gate,
        a_re=a_re, a_im=a_im, bm=_block_diag_b(bbt_re, bbt_im), cm=_block_diag_c(ssm_c_re[0], ssm_c_im[0]),
        ssm_d=row(ssm_d[0]), ssm_glu_w=ssm_glu_w[0].astype(BF16), ssm_glu_b=row(ssm_glu_b[0]),
        ssm_out_w=ssm_out_w[0].astype(BF16),
        conv_dw_w=conv_dw_w[0].astype(F32).reshape(CONV_WIDTH, CONV_SLABS, LANES).transpose(1, 0, 2),
        conv_dw_b=conv_dw_b[0].astype(F32).reshape(CONV_SLABS, 1, LANES),
        conv_ln_g=row(conv_ln_g[0]), conv_ln_b=row(conv_ln_b[0]), conv_out_w=conv_out_w[0].astype(BF16),
        w_o=w_o[0].astype(BF16),
        ffn2_norm=row(ffn2_norm[0]), ffn2_w1=ffn2_w1[0].astype(BF16), ffn2_w3=ffn2_w3[0].astype(BF16),
        ffn2_w2=(FFN_RES * ffn2_w2[0]).astype(BF16),
    )
    final_g = row(final_norm)

    nb_p, len_p, _ = x_prompt.shape
    y_p, re_p, im_p, conv_p = _trunk(
        x_prompt, p, final_g, jnp.zeros((nb_p, 2 * N_STATE), F32), jnp.zeros((HIST * nb_p, D_CONV), F32),
        tq=128, q=16)

    nb_s, len_s, _ = x_sample.shape
    cache_tm = cache_conv[0].astype(F32).transpose(1, 0, 2).reshape(HIST * nb_s, D_CONV)
    y_s, re_s, im_s, conv_s = _trunk(
        x_sample, p, final_g, _state_rows(state_ssm_re[0], state_ssm_im[0]), cache_tm, tq=len_s, q=len_s)

    return (y_p, y_s, re_p, im_p, conv_p, re_s, im_s, conv_s)
```

```python
import functools
import math

import jax
import jax.numpy as jnp
from jax import lax
from jax.experimental import pallas as pl
from jax.experimental.pallas import tpu as pltpu

D_MODEL = 1024
D_SSM = 512
SSM_GROUP = 16
SSM_GROUPS = 32
SSM_STATE = 64
D_CONV = 512
CONV_WIDTH = 31
D_FF = 2816
FFN_RES = 0.5
EPS = 1e-6

SUBLANES = 8
LANES = 128
MXU_DIM = 256
VMEM_BYTES = 64 * 1024 * 1024

N_STATE = SSM_GROUPS * SSM_STATE
SSM_HALVES = D_SSM // MXU_DIM
HALF_STATE = N_STATE // SSM_HALVES
SCAN_LANES = 512
CONV_SLABS = D_CONV // LANES
HIST = CONV_WIDTH - 1
N_PROJ = D_SSM + 2 * D_CONV
N_GATE_CHUNKS = 2 * D_MODEL // MXU_DIM

F32 = jnp.float32
BF16 = jnp.bfloat16


def _sigmoid(x):
    return 1.0 / (1.0 + jnp.exp(-x))


def _rms(x, g):
    return x * lax.rsqrt(jnp.mean(x * x, axis=-1, keepdims=True) + EPS) * g


def _dot(a, b):
    return jnp.dot(a, b, preferred_element_type=F32)


def _const_spec(shape):
    return pl.BlockSpec(shape, lambda *_: (0,) * len(shape), pipeline_mode=pl.Buffered(1))


def _discretize_kernel(lre_ref, lim_ref, dt_ref, bre_ref, bim_ref, are_ref, aim_ref, bbre_ref, bbim_ref):
    lre = lre_ref[...]
    lim = lim_ref[...]
    dt = jnp.exp(dt_ref[...])
    mag = jnp.exp(lre * dt)
    a_re = mag * jnp.cos(lim * dt)
    a_im = mag * jnp.sin(lim * dt)
    num_re = a_re - 1.0
    inv_den = 1.0 / (lre * lre + lim * lim)
    k_re = (num_re * lre + a_im * lim) * inv_den
    k_im = (a_im * lre - num_re * lim) * inv_den
    are_ref[...] = a_re
    aim_ref[...] = a_im
    bre = bre_ref[...]
    bim = bim_ref[...]
    bbre_ref[...] = k_re * bre - k_im * bim
    bbim_ref[...] = k_re * bim + k_im * bre


def _discretize(lam_re, lam_im, log_step, b_re, b_im):
    lre = lam_re.reshape(1, N_STATE).astype(F32)
    lim = lam_im.reshape(1, N_STATE).astype(F32)
    dt = jnp.repeat(log_step.astype(F32), SSM_STATE).reshape(1, N_STATE)
    bre = b_re.astype(F32).transpose(2, 0, 1).reshape(SSM_GROUP, N_STATE)
    bim = b_im.astype(F32).transpose(2, 0, 1).reshape(SSM_GROUP, N_STATE)
    row = jax.ShapeDtypeStruct((1, N_STATE), F32)
    mat = jax.ShapeDtypeStruct((SSM_GROUP, N_STATE), F32)
    return pl.pallas_call(_discretize_kernel, out_shape=(row, row, mat, mat), name="s5_discretize")(
        lre, lim, dt, bre, bim)


def _block_diag_b(bbt_re, bbt_im):
    gph = SSM_GROUPS // SSM_HALVES
    eye = jnp.eye(gph, dtype=F32)

    def one(bbt):
        b = bbt.reshape(SSM_GROUP, SSM_HALVES, gph, SSM_STATE)
        m = jnp.einsum("chgp,gk->hgckp", b, eye)
        return m.reshape(SSM_HALVES, MXU_DIM, HALF_STATE)

    return jnp.concatenate([one(bbt_re), one(bbt_im)], axis=-1).astype(BF16)


def _block_diag_c(c_re, c_im):
    gph = SSM_GROUPS // SSM_HALVES
    eye = jnp.eye(gph, dtype=F32)

    def one(c):
        cc = c.astype(F32).reshape(SSM_HALVES, gph, SSM_GROUP, SSM_STATE)
        m = jnp.einsum("hgcp,gk->hgpkc", cc, eye)
        return m.reshape(SSM_HALVES, HALF_STATE, MXU_DIM)

    return jnp.concatenate([one(c_re), -one(c_im)], axis=1).astype(BF16)


def _ffn_kernel(x_ref, g_ref, w1_ref, w3_ref, w2_ref, gf_ref, o_ref, *, ff_chunk, to_time_major):
    xb = x_ref[...]
    if not to_time_major:
        xb = jnp.transpose(xb, (1, 0, 2))
    nbg, tq, _ = xb.shape
    x = xb.reshape(nbg * tq, D_MODEL)
    h = _rms(x, g_ref[...]).astype(BF16)
    acc = jnp.zeros(x.shape, F32)
    for c in range(D_FF // ff_chunk):
        cols = slice(c * ff_chunk, (c + 1) * ff_chunk)
        a = _dot(h, w1_ref[:, cols])
        b = _dot(h, w3_ref[:, cols])
        gated = (a * _sigmoid(a) * b).astype(BF16)
        acc = acc + _dot(gated, w2_ref[cols, :])
    y = x + acc
    if not to_time_major:
        y = _rms(y, gf_ref[...])
    y = y.reshape(nbg, tq, D_MODEL)
    if to_time_major:
        y = jnp.transpose(y, (1, 0, 2))
    o_ref[...] = y


def _ffn(x, norm_g, w1, w3, w2, final_g, *, nb, length, tq, to_time_major):
    batch_major = pl.BlockSpec((SUBLANES, tq, D_MODEL), lambda g, i: (g, i, 0))
    time_major = pl.BlockSpec((tq, SUBLANES, D_MODEL), lambda g, i: (i, g, 0))
    if to_time_major:
        in_spec, out_spec = batch_major, time_major
        out_shape = jax.ShapeDtypeStruct((length, nb, D_MODEL), F32)
    else:
        in_spec, out_spec = time_major, batch_major
        out_shape = jax.ShapeDtypeStruct((nb, length, D_MODEL), F32)
    weight_bytes = 3 * D_MODEL * D_FF * 2
    block_bytes = SUBLANES * tq * D_MODEL * 4
    vmem = weight_bytes + 4 * block_bytes + 10 * block_bytes + (8 << 20)
    return pl.pallas_call(
        functools.partial(_ffn_kernel, ff_chunk=MXU_DIM, to_time_major=to_time_major),
        grid=(nb // SUBLANES, length // tq),
        in_specs=[in_spec, _const_spec((1, D_MODEL)), _const_spec((D_MODEL, D_FF)),
                  _const_spec((D_MODEL, D_FF)), _const_spec((D_FF, D_MODEL)), _const_spec((1, D_MODEL))],
        out_specs=out_spec,
        out_shape=out_shape,
        compiler_params=pltpu.CompilerParams(
            dimension_semantics=("arbitrary", "arbitrary"), vmem_limit_bytes=min(vmem, VMEM_BYTES - (6 << 20))),
        name="ffn_to_time_major" if to_time_major else "ffn_final",
    )(x, norm_g, w1, w3, w2, final_g)


def _mixer_kernel(x_ref, gn_ref, win_ref, wg_ref, are_ref, aim_ref, bm_ref, cm_ref, d_ref, gluw_ref, glub_ref,
                  sow_ref, cw_ref, cb_ref, lng_ref, lnb_ref, cow_ref, wo_ref, h0_ref, cache_ref,
                  o_ref, st_ref, cbuf_ref,
                  hb_scr, u_scr, bu_scr, vp_scr, conv_scr, cn_scr, gate_scr, y_scr, z_scr, part_scr, mix_scr,
                  *, nb, q):
    step = pl.program_id(0)
    rows = nb * q
    hist_rows = HIST * nb
    groups = nb // SUBLANES
    n_slabs = CONV_SLABS

    in_grid = step < pl.num_programs(0)

    def stage(fn):
        pl.when(in_grid)(fn)

    def conv_slab(s):
        taps = [jnp.broadcast_to(cw_ref[s, k:k + 1, :], (SUBLANES, LANES)) for k in range(CONV_WIDTH)]
        bias = jnp.broadcast_to(cb_ref[s], (SUBLANES, LANES))
        for bg in range(groups):
            acc = [bias] * q
            for t_in in range(q + HIST):
                x_in = vp_scr[s, t_in * nb + bg * SUBLANES:t_in * nb + (bg + 1) * SUBLANES, :]
                for t in range(max(0, t_in - HIST), min(q, t_in + 1)):
                    acc[t] = acc[t] + taps[t_in - t] * x_in
            for t in range(q):
                conv_scr[s, t * nb + bg * SUBLANES:t * nb + (bg + 1) * SUBLANES, :] = acc[t]

    def bu_piece(h):
        ub = u_scr[:, h * MXU_DIM:(h + 1) * MXU_DIM].astype(BF16)
        bu_scr[:, 2 * h * HALF_STATE:2 * (h + 1) * HALF_STATE] = _dot(ub, bm_ref[h])

    def glu_piece(c):
        cols = slice(c * MXU_DIM, (c + 1) * MXU_DIM)
        conv_a = _dot(hb_scr[...], win_ref[:, D_SSM + c * MXU_DIM:D_SSM + (c + 1) * MXU_DIM])
        conv_g = _dot(hb_scr[...], win_ref[:, D_SSM + D_CONV + c * MXU_DIM:D_SSM + D_CONV + (c + 1) * MXU_DIM])
        v = conv_a * _sigmoid(conv_g)
        for s in range(MXU_DIM // LANES):
            vp_scr[c * (MXU_DIM // LANES) + s, hist_rows:hist_rows + rows, :] = v[:, s * LANES:(s + 1) * LANES]

    def y_piece(h, row_blocks):
        for r in [slice(i * rows // row_blocks, (i + 1) * rows // row_blocks) for i in range(row_blocks)]:
            s = bu_scr[r, 2 * h * HALF_STATE:2 * (h + 1) * HALF_STATE].astype(BF16)
            y_scr[r, h * MXU_DIM:(h + 1) * MXU_DIM] = _dot(s, cm_ref[h])

    def scan_pass(h, j):
        re_cols = slice(2 * h * HALF_STATE + j * SCAN_LANES, 2 * h * HALF_STATE + (j + 1) * SCAN_LANES)
        im_cols = slice(re_cols.start + HALF_STATE, re_cols.stop + HALF_STATE)
        a_cols = slice(h * HALF_STATE + j * SCAN_LANES, h * HALF_STATE + (j + 1) * SCAN_LANES)
        ar = jnp.broadcast_to(are_ref[:, a_cols], (SUBLANES, SCAN_LANES))
        ai = jnp.broadcast_to(aim_ref[:, a_cols], (SUBLANES, SCAN_LANES))
        state = [(st_ref[bg * SUBLANES:(bg + 1) * SUBLANES, re_cols],
                  st_ref[bg * SUBLANES:(bg + 1) * SUBLANES, im_cols]) for bg in range(groups)]
        for t in range(q):
            for bg in range(groups):
                sr, si = state[bg]
                r = slice(t * nb + bg * SUBLANES, t * nb + (bg + 1) * SUBLANES)
                nr = ar * sr - ai * si + bu_scr[r, re_cols]
                ni = ar * si + ai * sr + bu_scr[r, im_cols]
                bu_scr[r, re_cols] = nr
                bu_scr[r, im_cols] = ni
                state[bg] = (nr, ni)
        for bg in range(groups):
            st_ref[bg * SUBLANES:(bg + 1) * SUBLANES, re_cols] = state[bg][0]
            st_ref[bg * SUBLANES:(bg + 1) * SUBLANES, im_cols] = state[bg][1]

    @pl.when(step == 0)
    def _():
        st_ref[...] = h0_ref[...]
        for s in range(n_slabs):
            vp_scr[s, 0:hist_rows, :] = cache_ref[:, s * LANES:(s + 1) * LANES]

    hb_scr[...] = _rms(x_ref[...], gn_ref[...]).astype(BF16)
    u_scr[...] = _dot(hb_scr[...], win_ref[:, 0:D_SSM])
    bu_piece(0)

    @stage
    def _():
        scan_pass(0, 0)
        bu_piece(1)

    @stage
    def _():
        scan_pass(0, 1)
        glu_piece(0)

    @stage
    def _():
        scan_pass(1, 0)
        glu_piece(1)

    @stage
    def _():
        scan_pass(1, 1)
        y_piece(0, 1)

    def conv_body(s, carry):
        conv_slab(s)
        return carry

    lax.fori_loop(0, n_slabs, conv_body, 0)

    @pl.when(step == pl.num_programs(0) - 1)
    def _():
        for s in range(n_slabs):
            cbuf_ref[:, s * LANES:(s + 1) * LANES] = vp_scr[s, rows:rows + hist_rows, :]

    for s in range(n_slabs):
        for r in range(0, hist_rows, nb):
            vp_scr[s, r:r + nb, :] = vp_scr[s, rows + r:rows + r + nb, :]

    ln_rows = rows // N_GATE_CHUNKS
    for c in range(N_GATE_CHUNKS):
        gate_scr[c] = _sigmoid(_dot(hb_scr[...], wg_ref[c]))
        r = slice(c * ln_rows, (c + 1) * ln_rows)
        cv = jnp.concatenate([conv_scr[s, r, :] for s in range(n_slabs)], axis=-1)
        xc = cv - jnp.mean(cv, axis=-1, keepdims=True)
        var = jnp.mean(xc * xc, axis=-1, keepdims=True)
        ln = xc * lax.rsqrt(var + EPS) * lng_ref[...] + lnb_ref[...]
        cn_scr[r, :] = (ln * _sigmoid(ln)).astype(BF16)

    y_piece(1, 2)
    n_out = D_MODEL // MXU_DIM
    for c in range(n_out):
        cols = slice(c * MXU_DIM, (c + 1) * MXU_DIM)
        part_scr[:, cols] = gate_scr[n_out + c] * _dot(cn_scr[...], cow_ref[:, cols])
        r = slice(c * rows // n_out, (c + 1) * rows // n_out)
        y = y_scr[r, :] + d_ref[...] * u_scr[r, :]
        z_scr[r, :] = 0.5 * y * (1.0 + jnp.tanh(math.sqrt(2.0 / math.pi) * (y + 0.044715 * (y * y * y))))

    z = z_scr[...]
    zb = (z * _sigmoid(_dot(z.astype(BF16), gluw_ref[...]) + glub_ref[...])).astype(BF16)
    for c in range(n_out):
        cols = slice(c * MXU_DIM, (c + 1) * MXU_DIM)
        mix_scr[:, cols] = (gate_scr[c] * _dot(zb, sow_ref[:, cols]) + part_scr[:, cols]).astype(BF16)
    o_ref[...] = x_ref[...] + _dot(mix_scr[...], wo_ref[...])


def _mixer(x_tm, p, h0, cache_tm, *, nb, length, q):
    rows = nb * q
    assert nb % SUBLANES == 0 and N_GATE_CHUNKS % CONV_SLABS == 0
    consts = [
        (p["mix_norm"], (1, D_MODEL)), (p["w_proj"], (D_MODEL, N_PROJ)), (p["w_gate"], (N_GATE_CHUNKS, D_MODEL, MXU_DIM)),
        (p["a_re"], (1, N_STATE)), (p["a_im"], (1, N_STATE)),
        (p["bm"], (SSM_HALVES, MXU_DIM, 2 * HALF_STATE)), (p["cm"], (SSM_HALVES, 2 * HALF_STATE, MXU_DIM)),
        (p["ssm_d"], (1, D_SSM)), (p["ssm_glu_w"], (D_SSM, D_SSM)), (p["ssm_glu_b"], (1, D_SSM)),
        (p["ssm_out_w"], (D_SSM, D_MODEL)),
        (p["conv_dw_w"], (CONV_SLABS, CONV_WIDTH, LANES)), (p["conv_dw_b"], (CONV_SLABS, 1, LANES)),
        (p["conv_ln_g"], (1, D_CONV)), (p["conv_ln_b"], (1, D_CONV)), (p["conv_out_w"], (D_CONV, D_MODEL)),
        (p["w_o"], (D_MODEL, D_MODEL)),
        (h0, (nb, 2 * N_STATE)), (cache_tm, (HIST * nb, D_CONV)),
    ]
    x_spec = pl.BlockSpec((rows, D_MODEL), lambda i: (i, 0))
    scratch_shapes = [
        ((rows, D_MODEL), BF16),
        ((rows, D_SSM), F32),
        ((rows, 2 * N_STATE), F32),
        ((CONV_SLABS, (HIST + q) * nb, LANES), F32),
        ((CONV_SLABS, rows, LANES), F32),
        ((rows, D_CONV), BF16),
        ((N_GATE_CHUNKS, rows, MXU_DIM), F32),
        ((rows, D_SSM), F32),
        ((rows, D_SSM), F32),
        ((rows, D_MODEL), F32),
        ((rows, D_MODEL), BF16),
    ]
    const_bytes = sum(math.prod(s) * a.dtype.itemsize for a, s in consts)
    scratch_bytes = sum(math.prod(s) * jnp.dtype(d).itemsize for s, d in scratch_shapes)
    block_bytes = rows * D_MODEL * 4
    vmem = const_bytes + scratch_bytes + 4 * block_bytes + 6 * block_bytes + (4 << 20)
    return pl.pallas_call(
        functools.partial(_mixer_kernel, nb=nb, q=q),
        grid=(length // q,),
        in_specs=[x_spec] + [_const_spec(s) for _, s in consts],
        out_specs=(x_spec, pl.BlockSpec((nb, 2 * N_STATE), lambda i: (0, 0)),
                   pl.BlockSpec((HIST * nb, D_CONV), lambda i: (0, 0))),
        out_shape=(jax.ShapeDtypeStruct((length * nb, D_MODEL), F32),
                   jax.ShapeDtypeStruct((nb, 2 * N_STATE), F32),
                   jax.ShapeDtypeStruct((HIST * nb, D_CONV), F32)),
        scratch_shapes=[pltpu.VMEM(s, d) for s, d in scratch_shapes],
        compiler_params=pltpu.CompilerParams(
            dimension_semantics=("arbitrary",), vmem_limit_bytes=min(vmem, VMEM_BYTES - (6 << 20))),
        name="mixer",
    )(x_tm, *[a for a, _ in consts])


def _trunk(x, p, final_g, h0, cache_tm, *, tq, q):
    nb, length, _ = x.shape
    x1 = _ffn(x, p["ffn1_norm"], p["ffn1_w1"], p["ffn1_w3"], p["ffn1_w2"], final_g,
              nb=nb, length=length, tq=tq, to_time_major=True)
    x2, state, conv_tail = _mixer(x1.reshape(length * nb, D_MODEL), p, h0, cache_tm, nb=nb, length=length, q=q)
    y = _ffn(x2.reshape(length, nb, D_MODEL), p["ffn2_norm"], p["ffn2_w1"], p["ffn2_w3"], p["ffn2_w2"], final_g,
             nb=nb, length=length, tq=tq, to_time_major=False)
    st = state.reshape(nb, SSM_HALVES, 2, SSM_GROUPS // SSM_HALVES, SSM_STATE)
    new_re = st[:, :, 0].reshape(1, nb, SSM_GROUPS, SSM_STATE)
    new_im = st[:, :, 1].reshape(1, nb, SSM_GROUPS, SSM_STATE)
    new_conv = conv_tail.reshape(HIST, nb, D_CONV).transpose(1, 0, 2)[None]
    return y, new_re, new_im, new_conv


def _state_rows(re, im):
    nb = re.shape[0]
    gph = SSM_GROUPS // SSM_HALVES
    st = jnp.stack([re.reshape(nb, SSM_HALVES, gph, SSM_STATE), im.reshape(nb, SSM_HALVES, gph, SSM_STATE)], axis=2)
    return st.reshape(nb, 2 * N_STATE).astype(F32)


def kernel(x_prompt, x_sample, state_ssm_re, state_ssm_im, cache_conv, ffn1_norm, ffn1_w1, ffn1_w3, ffn1_w2, mix_norm, w_in, ssm_lambda_re, ssm_lambda_im, ssm_log_step, ssm_b_re, ssm_b_im, ssm_c_re, ssm_c_im, ssm_d, ssm_glu_w, ssm_glu_b, ssm_out_w, conv_dw_w, conv_dw_b, conv_ln_g, conv_ln_b, conv_out_w, w_o, ffn2_norm, ffn2_w1, ffn2_w3, ffn2_w2, final_norm):
    a_re, a_im, bbt_re, bbt_im = _discretize(ssm_lambda_re[0], ssm_lambda_im[0], ssm_log_step[0],
                                             ssm_b_re[0], ssm_b_im[0])
    row = lambda v: v.reshape(1, -1).astype(F32)
    w_in_b = w_in[0].astype(BF16)
    w_gate = w_in_b[:, N_PROJ:].reshape(D_MODEL, N_GATE_CHUNKS, MXU_DIM).transpose(1, 0, 2)
    p = dict(
        ffn1_norm=row(ffn1_norm[0]), ffn1_w1=ffn1_w1[0].astype(BF16), ffn1_w3=ffn1_w3[0].astype(BF16),
        ffn1_w2=(FFN_RES * ffn1_w2[0]).astype(BF16),
        mix_norm=row(mix_norm[0]), w_proj=w_in_b[:, :N_PROJ], w_gate=w_gate,
        a_re=a_re, a_im=a_im, bm=_block_diag_b(bbt_re, bbt_im), cm=_block_diag_c(ssm_c_re[0], ssm_c_im[0]),
        ssm_d=row(ssm_d[0]), ssm_glu_w=ssm_glu_w[0].astype(BF16), ssm_glu_b=row(ssm_glu_b[0]),
        ssm_out_w=ssm_out_w[0].astype(BF16),
        conv_dw_w=conv_dw_w[0].astype(F32).reshape(CONV_WIDTH, CONV_SLABS, LANES).transpose(1, 0, 2),
        conv_dw_b=conv_dw_b[0].astype(F32).reshape(CONV_SLABS, 1, LANES),
        conv_ln_g=row(conv_ln_g[0]), conv_ln_b=row(conv_ln_b[0]), conv_out_w=conv_out_w[0].astype(BF16),
        w_o=w_o[0].astype(BF16),
        ffn2_norm=row(ffn2_norm[0]), ffn2_w1=ffn2_w1[0].astype(BF16), ffn2_w3=ffn2_w3[0].astype(BF16),
        ffn2_w2=(FFN_RES * ffn2_w2[0]).astype(BF16),
    )
    final_g = row(final_norm)

    nb_p, len_p, _ = x_prompt.shape
    y_p, re_p, im_p, conv_p = _trunk(
        x_prompt, p, final_g, jnp.zeros((nb_p, 2 * N_STATE), F32), jnp.zeros((HIST * nb_p, D_CONV), F32),
        tq=128, q=16)

    nb_s, len_s, _ = x_sample.shape
    cache_tm = cache_conv[0].astype(F32).transpose(1, 0, 2).reshape(HIST * nb_s, D_CONV)
    y_s, re_s, im_s, conv_s = _trunk(
        x_sample, p, final_g, _state_rows(state_ssm_re[0], state_ssm_im[0]), cache_tm, tq=len_s, q=len_s)

    return (y_p, y_s, re_p, im_p, conv_p, re_s, im_s, conv_s)
```

```python
import functools
import math

import jax
import jax.numpy as jnp
from jax import lax
from jax.experimental import pallas as pl
from jax.experimental.pallas import tpu as pltpu

D_MODEL = 1024
D_SSM = 512
SSM_GROUP = 16
SSM_GROUPS = 32
SSM_STATE = 64
D_CONV = 512
CONV_WIDTH = 31
D_FF = 2816
FFN_RES = 0.5
EPS = 1e-6

SUBLANES = 8
LANES = 128
MXU_DIM = 256
VMEM_BYTES = 64 * 1024 * 1024

N_STATE = SSM_GROUPS * SSM_STATE
SSM_HALVES = D_SSM // MXU_DIM
HALF_STATE = N_STATE // SSM_HALVES
SCAN_LANES = 512
CONV_SLABS = D_CONV // LANES
HIST = CONV_WIDTH - 1
N_PROJ = D_SSM + 2 * D_CONV
N_GATE_CHUNKS = 2 * D_MODEL // MXU_DIM

F32 = jnp.float32
BF16 = jnp.bfloat16


def _sigmoid(x):
    return 1.0 / (1.0 + jnp.exp(-x))


def _rms(x, g):
    return x * lax.rsqrt(jnp.mean(x * x, axis=-1, keepdims=True) + EPS) * g


def _dot(a, b):
    return jnp.dot(a, b, preferred_element_type=F32)


def _const_spec(shape):
    return pl.BlockSpec(shape, lambda *_: (0,) * len(shape), pipeline_mode=pl.Buffered(1))


def _discretize_kernel(lre_ref, lim_ref, dt_ref, bre_ref, bim_ref, are_ref, aim_ref, bbre_ref, bbim_ref):
    lre = lre_ref[...]
    lim = lim_ref[...]
    dt = jnp.exp(dt_ref[...])
    mag = jnp.exp(lre * dt)
    a_re = mag * jnp.cos(lim * dt)
    a_im = mag * jnp.sin(lim * dt)
    num_re = a_re - 1.0
    inv_den = 1.0 / (lre * lre + lim * lim)
    k_re = (num_re * lre + a_im * lim) * inv_den
    k_im = (a_im * lre - num_re * lim) * inv_den
    are_ref[...] = a_re
    aim_ref[...] = a_im
    bre = bre_ref[...]
    bim = bim_ref[...]
    bbre_ref[...] = k_re * bre - k_im * bim
    bbim_ref[...] = k_re * bim + k_im * bre


def _discretize(lam_re, lam_im, log_step, b_re, b_im):
    lre = lam_re.reshape(1, N_STATE).astype(F32)
    lim = lam_im.reshape(1, N_STATE).astype(F32)
    dt = jnp.repeat(log_step.astype(F32), SSM_STATE).reshape(1, N_STATE)
    bre = b_re.astype(F32).transpose(2, 0, 1).reshape(SSM_GROUP, N_STATE)
    bim = b_im.astype(F32).transpose(2, 0, 1).reshape(SSM_GROUP, N_STATE)
    row = jax.ShapeDtypeStruct((1, N_STATE), F32)
    mat = jax.ShapeDtypeStruct((SSM_GROUP, N_STATE), F32)
    return pl.pallas_call(_discretize_kernel, out_shape=(row, row, mat, mat), name="s5_discretize")(
        lre, lim, dt, bre, bim)


def _block_diag_b(bbt_re, bbt_im):
    gph = SSM_GROUPS // SSM_HALVES
    eye = jnp.eye(gph, dtype=F32)

    def one(bbt):
        b = bbt.reshape(SSM_GROUP, SSM_HALVES, gph, SSM_STATE)
        m = jnp.einsum("chgp,gk->hgckp", b, eye)
        return m.reshape(SSM_HALVES, MXU_DIM, HALF_STATE)

    return jnp.concatenate([one(bbt_re), one(bbt_im)], axis=-1).astype(BF16)


def _block_diag_c(c_re, c_im):
    gph = SSM_GROUPS // SSM_HALVES
    eye = jnp.eye(gph, dtype=F32)

    def one(c):
        cc = c.astype(F32).reshape(SSM_HALVES, gph, SSM_GROUP, SSM_STATE)
        m = jnp.einsum("hgcp,gk->hgpkc", cc, eye)
        return m.reshape(SSM_HALVES, HALF_STATE, MXU_DIM)

    return jnp.concatenate([one(c_re), -one(c_im)], axis=1).astype(BF16)


def _ffn_kernel(x_ref, g_ref, w1_ref, w3_ref, w2_ref, gf_ref, o_ref, *, ff_chunk, to_time_major):
    xb = x_ref[...]
    if not to_time_major:
        xb = jnp.transpose(xb, (1, 0, 2))
    nbg, tq, _ = xb.shape
    x = xb.reshape(nbg * tq, D_MODEL)
    h = _rms(x, g_ref[...]).astype(BF16)
    acc = jnp.zeros(x.shape, F32)
    for c in range(D_FF // ff_chunk):
        cols = slice(c * ff_chunk, (c + 1) * ff_chunk)
        a = _dot(h, w1_ref[:, cols])
        b = _dot(h, w3_ref[:, cols])
        gated = (a * _sigmoid(a) * b).astype(BF16)
        acc = acc + _dot(gated, w2_ref[cols, :])
    y = x + acc
    if not to_time_major:
        y = _rms(y, gf_ref[...])
    y = y.reshape(nbg, tq, D_MODEL)
    if to_time_major:
        y = jnp.transpose(y, (1, 0, 2))
    o_ref[...] = y


def _ffn(x, norm_g, w1, w3, w2, final_g, *, nb, length, tq, to_time_major):
    batch_major = pl.BlockSpec((SUBLANES, tq, D_MODEL), lambda g, i: (g, i, 0))
    time_major = pl.BlockSpec((tq, SUBLANES, D_MODEL), lambda g, i: (i, g, 0))
    if to_time_major:
        in_spec, out_spec = batch_major, time_major
        out_shape = jax.ShapeDtypeStruct((length, nb, D_MODEL), F32)
    else:
        in_spec, out_spec = time_major, batch_major
        out_shape = jax.ShapeDtypeStruct((nb, length, D_MODEL), F32)
    weight_bytes = 3 * D_MODEL * D_FF * 2
    block_bytes = SUBLANES * tq * D_MODEL * 4
    vmem = weight_bytes + 4 * block_bytes + 10 * block_bytes + (8 << 20)
    return pl.pallas_call(
        functools.partial(_ffn_kernel, ff_chunk=MXU_DIM, to_time_major=to_time_major),
        grid=(nb // SUBLANES, length // tq),
        in_specs=[in_spec, _const_spec((1, D_MODEL)), _const_spec((D_MODEL, D_FF)),
                  _const_spec((D_MODEL, D_FF)), _const_spec((D_FF, D_MODEL)), _const_spec((1, D_MODEL))],
        out_specs=out_spec,
        out_shape=out_shape,
        compiler_params=pltpu.CompilerParams(
            dimension_semantics=("arbitrary", "arbitrary"), vmem_limit_bytes=min(vmem, VMEM_BYTES - (6 << 20))),
        name="ffn_to_time_major" if to_time_major else "ffn_final",
    )(x, norm_g, w1, w3, w2, final_g)


def _mixer_kernel(x_ref, gn_ref, win_ref, are_ref, aim_ref, bm_ref, cm_ref, d_ref, gluw_ref, glub_ref,
                  sow_ref, cw_ref, cb_ref, lng_ref, lnb_ref, cow_ref, wo_ref, h0_ref, cache_ref,
                  o_ref, st_ref, cbuf_ref,
                  hb_scr, u_scr, bu_scr, vp_scr, conv_scr, cn_scr, gate_scr, y_scr, z_scr, part_scr, mix_scr,
                  *, nb, q):
    step = pl.program_id(0)
    rows = nb * q
    hist_rows = HIST * nb
    groups = nb // SUBLANES
    n_slabs = CONV_SLABS

    in_grid = step < pl.num_programs(0)

    def stage(fn):
        pl.when(in_grid)(fn)

    def conv_slab(s):
        taps = [jnp.broadcast_to(cw_ref[s, k:k + 1, :], (SUBLANES, LANES)) for k in range(CONV_WIDTH)]
        bias = jnp.broadcast_to(cb_ref[s], (SUBLANES, LANES))
        for bg in range(groups):
            acc = [bias] * q
            for t_in in range(q + HIST):
                x_in = vp_scr[s, t_in * nb + bg * SUBLANES:t_in * nb + (bg + 1) * SUBLANES, :]
                for t in range(max(0, t_in - HIST), min(q, t_in + 1)):
                    acc[t] = acc[t] + taps[t_in - t] * x_in
            for t in range(q):
                conv_scr[s, t * nb + bg * SUBLANES:t * nb + (bg + 1) * SUBLANES, :] = acc[t]

    def bu_piece(h):
        ub = u_scr[:, h * MXU_DIM:(h + 1) * MXU_DIM].astype(BF16)
        bu_scr[:, 2 * h * HALF_STATE:2 * (h + 1) * HALF_STATE] = _dot(ub, bm_ref[h])

    def glu_piece(c):
        cols = slice(c * MXU_DIM, (c + 1) * MXU_DIM)
        conv_a = _dot(hb_scr[...], win_ref[:, D_SSM + c * MXU_DIM:D_SSM + (c + 1) * MXU_DIM])
        conv_g = _dot(hb_scr[...], win_ref[:, D_SSM + D_CONV + c * MXU_DIM:D_SSM + D_CONV + (c + 1) * MXU_DIM])
        v = conv_a * _sigmoid(conv_g)
        for s in range(MXU_DIM // LANES):
            vp_scr[c * (MXU_DIM // LANES) + s, hist_rows:hist_rows + rows, :] = v[:, s * LANES:(s + 1) * LANES]

    def y_piece(h, row_blocks):
        for r in [slice(i * rows // row_blocks, (i + 1) * rows // row_blocks) for i in range(row_blocks)]:
            s = bu_scr[r, 2 * h * HALF_STATE:2 * (h + 1) * HALF_STATE].astype(BF16)
            y_scr[r, h * MXU_DIM:(h + 1) * MXU_DIM] = _dot(s, cm_ref[h])

    def scan_pass(h, j):
        re_cols = slice(2 * h * HALF_STATE + j * SCAN_LANES, 2 * h * HALF_STATE + (j + 1) * SCAN_LANES)
        im_cols = slice(re_cols.start + HALF_STATE, re_cols.stop + HALF_STATE)
        a_cols = slice(h * HALF_STATE + j * SCAN_LANES, h * HALF_STATE + (j + 1) * SCAN_LANES)
        ar = jnp.broadcast_to(are_ref[:, a_cols], (SUBLANES, SCAN_LANES))
        ai = jnp.broadcast_to(aim_ref[:, a_cols], (SUBLANES, SCAN_LANES))
        state = [(st_ref[bg * SUBLANES:(bg + 1) * SUBLANES, re_cols],
                  st_ref[bg * SUBLANES:(bg + 1) * SUBLANES, im_cols]) for bg in range(groups)]
        for t in range(q):
            for bg in range(groups):
                sr, si = state[bg]
                r = slice(t * nb + bg * SUBLANES, t * nb + (bg + 1) * SUBLANES)
                nr = ar * sr - ai * si + bu_scr[r, re_cols]
                ni = ar * si + ai * sr + bu_scr[r, im_cols]
                bu_scr[r, re_cols] = nr
                bu_scr[r, im_cols] = ni
                state[bg] = (nr, ni)
        for bg in range(groups):
            st_ref[bg * SUBLANES:(bg + 1) * SUBLANES, re_cols] = state[bg][0]
            st_ref[bg * SUBLANES:(bg + 1) * SUBLANES, im_cols] = state[bg][1]

    @pl.when(step == 0)
    def _():
        st_ref[...] = h0_ref[...]
        for s in range(n_slabs):
            vp_scr[s, 0:hist_rows, :] = cache_ref[:, s * LANES:(s + 1) * LANES]

    hb_scr[...] = _rms(x_ref[...], gn_ref[...]).astype(BF16)
    u_scr[...] = _dot(hb_scr[...], win_ref[:, 0:D_SSM])
    bu_piece(0)

    @stage
    def _():
        scan_pass(0, 0)
        bu_piece(1)

    @stage
    def _():
        scan_pass(0, 1)
        glu_piece(0)

    @stage
    def _():
        scan_pass(1, 0)
        glu_piece(1)

    @stage
    def _():
        scan_pass(1, 1)
        y_piece(0, 1)

    def conv_body(s, carry):
        conv_slab(s)
        return carry

    lax.fori_loop(0, n_slabs, conv_body, 0)

    @pl.when(step == pl.num_programs(0) - 1)
    def _():
        for s in range(n_slabs):
            cbuf_ref[:, s * LANES:(s + 1) * LANES] = vp_scr[s, rows:rows + hist_rows, :]

    for s in range(n_slabs):
        for r in range(0, hist_rows, nb):
            vp_scr[s, r:r + nb, :] = vp_scr[s, rows + r:rows + r + nb, :]

    ln_rows = rows // N_GATE_CHUNKS
    for c in range(N_GATE_CHUNKS):
        gate_cols = slice(N_PROJ + c * MXU_DIM, N_PROJ + (c + 1) * MXU_DIM)
        gate_scr[c] = _sigmoid(_dot(hb_scr[...], win_ref[:, gate_cols]))
        r = slice(c * ln_rows, (c + 1) * ln_rows)
        cv = jnp.concatenate([conv_scr[s, r, :] for s in range(n_slabs)], axis=-1)
        xc = cv - jnp.mean(cv, axis=-1, keepdims=True)
        var = jnp.mean(xc * xc, axis=-1, keepdims=True)
        ln = xc * lax.rsqrt(var + EPS) * lng_ref[...] + lnb_ref[...]
        cn_scr[r, :] = (ln * _sigmoid(ln)).astype(BF16)

    y_piece(1, 2)
    n_out = D_MODEL // MXU_DIM
    for c in range(n_out):
        cols = slice(c * MXU_DIM, (c + 1) * MXU_DIM)
        part_scr[:, cols] = gate_scr[n_out + c] * _dot(cn_scr[...], cow_ref[:, cols])
        r = slice(c * rows // n_out, (c + 1) * rows // n_out)
        y = y_scr[r, :] + d_ref[...] * u_scr[r, :]
        z_scr[r, :] = 0.5 * y * (1.0 + jnp.tanh(math.sqrt(2.0 / math.pi) * (y + 0.044715 * (y * y * y))))

    z = z_scr[...]
    zb = (z * _sigmoid(_dot(z.astype(BF16), gluw_ref[...]) + glub_ref[...])).astype(BF16)
    for c in range(n_out):
        cols = slice(c * MXU_DIM, (c + 1) * MXU_DIM)
        mix_scr[:, cols] = (gate_scr[c] * _dot(zb, sow_ref[:, cols]) + part_scr[:, cols]).astype(BF16)
    o_ref[...] = x_ref[...] + _dot(mix_scr[...], wo_ref[...])


def _mixer(x_tm, p, h0, cache_tm, *, nb, length, q):
    rows = nb * q
    assert nb % SUBLANES == 0 and N_GATE_CHUNKS % CONV_SLABS == 0
    consts = [
        (p["mix_norm"], (1, D_MODEL)), (p["w_in"], (D_MODEL, N_PROJ + N_GATE_CHUNKS * MXU_DIM)),
        (p["a_re"], (1, N_STATE)), (p["a_im"], (1, N_STATE)),
        (p["bm"], (SSM_HALVES, MXU_DIM, 2 * HALF_STATE)), (p["cm"], (SSM_HALVES, 2 * HALF_STATE, MXU_DIM)),
        (p["ssm_d"], (1, D_SSM)), (p["ssm_glu_w"], (D_SSM, D_SSM)), (p["ssm_glu_b"], (1, D_SSM)),
        (p["ssm_out_w"], (D_SSM, D_MODEL)),
        (p["conv_dw_w"], (CONV_SLABS, CONV_WIDTH, LANES)), (p["conv_dw_b"], (CONV_SLABS, 1, LANES)),
        (p["conv_ln_g"], (1, D_CONV)), (p["conv_ln_b"], (1, D_CONV)), (p["conv_out_w"], (D_CONV, D_MODEL)),
        (p["w_o"], (D_MODEL, D_MODEL)),
        (h0, (nb, 2 * N_STATE)), (cache_tm, (HIST * nb, D_CONV)),
    ]
    x_spec = pl.BlockSpec((rows, D_MODEL), lambda i: (i, 0))
    scratch_shapes = [
        ((rows, D_MODEL), BF16),
        ((rows, D_SSM), F32),
        ((rows, 2 * N_STATE), F32),
        ((CONV_SLABS, (HIST + q) * nb, LANES), F32),
        ((CONV_SLABS, rows, LANES), F32),
        ((rows, D_CONV), BF16),
        ((N_GATE_CHUNKS, rows, MXU_DIM), F32),
        ((rows, D_SSM), F32),
        ((rows, D_SSM), F32),
        ((rows, D_MODEL), F32),
        ((rows, D_MODEL), BF16),
    ]
    const_bytes = sum(math.prod(s) * a.dtype.itemsize for a, s in consts)
    scratch_bytes = sum(math.prod(s) * jnp.dtype(d).itemsize for s, d in scratch_shapes)
    block_bytes = rows * D_MODEL * 4
    vmem = const_bytes + scratch_bytes + 4 * block_bytes + 6 * block_bytes + (4 << 20)
    return pl.pallas_call(
        functools.partial(_mixer_kernel, nb=nb, q=q),
        grid=(length // q,),
        in_specs=[x_spec] + [_const_spec(s) for _, s in consts],
        out_specs=(x_spec, pl.BlockSpec((nb, 2 * N_STATE), lambda i: (0, 0)),
                   pl.BlockSpec((HIST * nb, D_CONV), lambda i: (0, 0))),
        out_shape=(jax.ShapeDtypeStruct((length * nb, D_MODEL), F32),
                   jax.ShapeDtypeStruct((nb, 2 * N_STATE), F32),
                   jax.ShapeDtypeStruct((HIST * nb, D_CONV), F32)),
        scratch_shapes=[pltpu.VMEM(s, d) for s, d in scratch_shapes],
        compiler_params=pltpu.CompilerParams(
            dimension_semantics=("arbitrary",), vmem_limit_bytes=min(vmem, VMEM_BYTES - (6 << 20))),
        name="mixer",
    )(x_tm, *[a for a, _ in consts])


def _trunk(x, p, final_g, h0, cache_tm, *, tq, q):
    nb, length, _ = x.shape
    x1 = _ffn(x, p["ffn1_norm"], p["ffn1_w1"], p["ffn1_w3"], p["ffn1_w2"], final_g,
              nb=nb, length=length, tq=tq, to_time_major=True)
    x2, state, conv_tail = _mixer(x1.reshape(length * nb, D_MODEL), p, h0, cache_tm, nb=nb, length=length, q=q)
    y = _ffn(x2.reshape(length, nb, D_MODEL), p["ffn2_norm"], p["ffn2_w1"], p["ffn2_w3"], p["ffn2_w2"], final_g,
             nb=nb, length=length, tq=tq, to_time_major=False)
    st = state.reshape(nb, SSM_HALVES, 2, SSM_GROUPS // SSM_HALVES, SSM_STATE)
    new_re = st[:, :, 0].reshape(1, nb, SSM_GROUPS, SSM_STATE)
    new_im = st[:, :, 1].reshape(1, nb, SSM_GROUPS, SSM_STATE)
    new_conv = conv_tail.reshape(HIST, nb, D_CONV).transpose(1, 0, 2)[None]
    return y, new_re, new_im, new_conv


def _state_rows(re, im):
    nb = re.shape[0]
    gph = SSM_GROUPS // SSM_HALVES
    st = jnp.stack([re.reshape(nb, SSM_HALVES, gph, SSM_STATE), im.reshape(nb, SSM_HALVES, gph, SSM_STATE)], axis=2)
    return st.reshape(nb, 2 * N_STATE).astype(F32)


def kernel(x_prompt, x_sample, state_ssm_re, state_ssm_im, cache_conv, ffn1_norm, ffn1_w1, ffn1_w3, ffn1_w2, mix_norm, w_in, ssm_lambda_re, ssm_lambda_im, ssm_log_step, ssm_b_re, ssm_b_im, ssm_c_re, ssm_c_im, ssm_d, ssm_glu_w, ssm_glu_b, ssm_out_w, conv_dw_w, conv_dw_b, conv_ln_g, conv_ln_b, conv_out_w, w_o, ffn2_norm, ffn2_w1, ffn2_w3, ffn2_w2, final_norm):
    a_re, a_im, bbt_re, bbt_im = _discretize(ssm_lambda_re[0], ssm_lambda_im[0], ssm_log_step[0],
                                             ssm_b_re[0], ssm_b_im[0])
    row = lambda v: v.reshape(1, -1).astype(F32)
    p = dict(
        ffn1_norm=row(ffn1_norm[0]), ffn1_w1=ffn1_w1[0].astype(BF16), ffn1_w3=ffn1_w3[0].astype(BF16),
        ffn1_w2=(FFN_RES * ffn1_w2[0]).astype(BF16),
        mix_norm=row(mix_norm[0]), w_in=w_in[0].astype(BF16),
        a_re=a_re, a_im=a_im, bm=_block_diag_b(bbt_re, bbt_im), cm=_block_diag_c(ssm_c_re[0], ssm_c_im[0]),
        ssm_d=row(ssm_d[0]), ssm_glu_w=ssm_glu_w[0].astype(BF16), ssm_glu_b=row(ssm_glu_b[0]),
        ssm_out_w=ssm_out_w[0].astype(BF16),
        conv_dw_w=conv_dw_w[0].astype(F32).reshape(CONV_WIDTH, CONV_SLABS, LANES).transpose(1, 0, 2),
        conv_dw_b=conv_dw_b[0].astype(F32).reshape(CONV_SLABS, 1, LANES),
        conv_ln_g=row(conv_ln_g[0]), conv_ln_b=row(conv_ln_b[0]), conv_out_w=conv_out_w[0].astype(BF16),
        w_o=w_o[0].astype(BF16),
        ffn2_norm=row(ffn2_norm[0]), ffn2_w1=ffn2_w1[0].astype(BF16), ffn2_w3=ffn2_w3[0].astype(BF16),
        ffn2_w2=(FFN_RES * ffn2_w2[0]).astype(BF16),
    )
    final_g = row(final_norm)

    nb_p, len_p, _ = x_prompt.shape
    y_p, re_p, im_p, conv_p = _trunk(
        x_prompt, p, final_g, jnp.zeros((nb_p, 2 * N_STATE), F32), jnp.zeros((HIST * nb_p, D_CONV), F32),
        tq=128, q=16)

    nb_s, len_s, _ = x_sample.shape
    cache_tm = cache_conv[0].astype(F32).transpose(1, 0, 2).reshape(HIST * nb_s, D_CONV)
    y_s, re_s, im_s, conv_s = _trunk(
        x_sample, p, final_g, _state_rows(state_ssm_re[0], state_ssm_im[0]), cache_tm, tq=len_s, q=len_s)

    return (y_p, y_s, re_p, im_p, conv_p, re_s, im_s, conv_s)
```

```python
import functools
import math

import jax
import jax.numpy as jnp
from jax import lax
from jax.experimental import pallas as pl
from jax.experimental.pallas import tpu as pltpu

D_MODEL = 1024
D_SSM = 512
SSM_GROUP = 16
SSM_GROUPS = 32
SSM_STATE = 64
D_CONV = 512
CONV_WIDTH = 31
D_FF = 2816
FFN_RES = 0.5
EPS = 1e-6

SUBLANES = 8
LANES = 128
MXU_DIM = 256
VMEM_BYTES = 64 * 1024 * 1024

N_STATE = SSM_GROUPS * SSM_STATE
SSM_HALVES = D_SSM // MXU_DIM
HALF_STATE = N_STATE // SSM_HALVES
SCAN_LANES = 512
CONV_SLABS = D_CONV // LANES
HIST = CONV_WIDTH - 1
N_PROJ = D_SSM + 2 * D_CONV
N_GATE_CHUNKS = 2 * D_MODEL // MXU_DIM

F32 = jnp.float32
BF16 = jnp.bfloat16


def _sigmoid(x):
    return 1.0 / (1.0 + jnp.exp(-x))


def _rms(x, g):
    return x * lax.rsqrt(jnp.mean(x * x, axis=-1, keepdims=True) + EPS) * g


def _dot(a, b):
    return jnp.dot(a, b, preferred_element_type=F32)


def _const_spec(shape):
    return pl.BlockSpec(shape, lambda *_: (0,) * len(shape), pipeline_mode=pl.Buffered(1))


def _discretize_kernel(lre_ref, lim_ref, dt_ref, bre_ref, bim_ref, are_ref, aim_ref, bbre_ref, bbim_ref):
    lre = lre_ref[...]
    lim = lim_ref[...]
    dt = jnp.exp(dt_ref[...])
    mag = jnp.exp(lre * dt)
    a_re = mag * jnp.cos(lim * dt)
    a_im = mag * jnp.sin(lim * dt)
    num_re = a_re - 1.0
    inv_den = 1.0 / (lre * lre + lim * lim)
    k_re = (num_re * lre + a_im * lim) * inv_den
    k_im = (a_im * lre - num_re * lim) * inv_den
    are_ref[...] = a_re
    aim_ref[...] = a_im
    bre = bre_ref[...]
    bim = bim_ref[...]
    bbre_ref[...] = k_re * bre - k_im * bim
    bbim_ref[...] = k_re * bim + k_im * bre


def _discretize(lam_re, lam_im, log_step, b_re, b_im):
    lre = lam_re.reshape(1, N_STATE).astype(F32)
    lim = lam_im.reshape(1, N_STATE).astype(F32)
    dt = jnp.repeat(log_step.astype(F32), SSM_STATE).reshape(1, N_STATE)
    bre = b_re.astype(F32).transpose(2, 0, 1).reshape(SSM_GROUP, N_STATE)
    bim = b_im.astype(F32).transpose(2, 0, 1).reshape(SSM_GROUP, N_STATE)
    row = jax.ShapeDtypeStruct((1, N_STATE), F32)
    mat = jax.ShapeDtypeStruct((SSM_GROUP, N_STATE), F32)
    return pl.pallas_call(_discretize_kernel, out_shape=(row, row, mat, mat), name="s5_discretize")(
        lre, lim, dt, bre, bim)


def _block_diag_b(bbt_re, bbt_im):
    gph = SSM_GROUPS // SSM_HALVES
    eye = jnp.eye(gph, dtype=F32)

    def one(bbt):
        b = bbt.reshape(SSM_GROUP, SSM_HALVES, gph, SSM_STATE)
        m = jnp.einsum("chgp,gk->hgckp", b, eye)
        return m.reshape(SSM_HALVES, MXU_DIM, HALF_STATE)

    return jnp.concatenate([one(bbt_re), one(bbt_im)], axis=-1).astype(BF16)


def _block_diag_c(c_re, c_im):
    gph = SSM_GROUPS // SSM_HALVES
    eye = jnp.eye(gph, dtype=F32)

    def one(c):
        cc = c.astype(F32).reshape(SSM_HALVES, gph, SSM_GROUP, SSM_STATE)
        m = jnp.einsum("hgcp,gk->hgpkc", cc, eye)
        return m.reshape(SSM_HALVES, HALF_STATE, MXU_DIM)

    return jnp.concatenate([one(c_re), -one(c_im)], axis=1).astype(BF16)


def _ffn_kernel(x_ref, g_ref, w1_ref, w3_ref, w2_ref, gf_ref, o_ref, *scratch, ff_chunk, to_time_major):
    if to_time_major:
        xbuf, sems = scratch
        tq = xbuf.shape[1]
        g, i = pl.program_id(0), pl.program_id(1)
        n_i = pl.num_programs(1)
        step = g * n_i + i
        last = pl.num_programs(0) * n_i - 1
        slot = lax.rem(step, 2)

        def tile_copies(gg, ii, ss):
            return [pltpu.make_async_copy(x_ref.at[gg * SUBLANES + b, pl.ds(ii * tq, tq), :],
                                          xbuf.at[ss, :, b, :], sems.at[ss, b]) for b in range(SUBLANES)]

        @pl.when(step == 0)
        def _():
            for cp in tile_copies(g, i, slot):
                cp.start()

        @pl.when(step < last)
        def _():
            wrap = i + 1 == n_i
            for cp in tile_copies(jnp.where(wrap, g + 1, g), jnp.where(wrap, 0, i + 1), 1 - slot):
                cp.start()

        for cp in tile_copies(g, i, slot):
            cp.wait()
        xb = xbuf[slot]
        n0, n1 = tq, SUBLANES
    else:
        xb = jnp.transpose(x_ref[...], (1, 0, 2))
        n0, n1, _ = xb.shape
    x = xb.reshape(n0 * n1, D_MODEL)
    h = _rms(x, g_ref[...]).astype(BF16)
    acc = jnp.zeros(x.shape, F32)
    for c in range(D_FF // ff_chunk):
        cols = slice(c * ff_chunk, (c + 1) * ff_chunk)
        a = _dot(h, w1_ref[:, cols])
        b = _dot(h, w3_ref[:, cols])
        gated = (a * _sigmoid(a) * b).astype(BF16)
        acc = acc + _dot(gated, w2_ref[cols, :])
    y = x + acc
    if not to_time_major:
        y = _rms(y, gf_ref[...])
    o_ref[...] = y.reshape(n0, n1, D_MODEL)


def _ffn(x, norm_g, w1, w3, w2, final_g, *, nb, length, tq, to_time_major):
    batch_major = pl.BlockSpec((SUBLANES, tq, D_MODEL), lambda g, i: (g, i, 0))
    time_major = pl.BlockSpec((tq, SUBLANES, D_MODEL), lambda g, i: (i, g, 0))
    if to_time_major:
        in_spec, out_spec = pl.BlockSpec(memory_space=pl.ANY), time_major
        out_shape = jax.ShapeDtypeStruct((length, nb, D_MODEL), F32)
        scratch = [pltpu.VMEM((2, tq, SUBLANES, D_MODEL), F32),
                   pltpu.SemaphoreType.DMA((2, SUBLANES))]
    else:
        in_spec, out_spec = time_major, batch_major
        out_shape = jax.ShapeDtypeStruct((nb, length, D_MODEL), F32)
        scratch = []
    weight_bytes = 3 * D_MODEL * D_FF * 2
    block_bytes = SUBLANES * tq * D_MODEL * 4
    vmem = weight_bytes + 4 * block_bytes + 10 * block_bytes + (8 << 20)
    return pl.pallas_call(
        functools.partial(_ffn_kernel, ff_chunk=MXU_DIM, to_time_major=to_time_major),
        grid=(nb // SUBLANES, length // tq),
        in_specs=[in_spec, _const_spec((1, D_MODEL)), _const_spec((D_MODEL, D_FF)),
                  _const_spec((D_MODEL, D_FF)), _const_spec((D_FF, D_MODEL)), _const_spec((1, D_MODEL))],
        out_specs=out_spec,
        out_shape=out_shape,
        scratch_shapes=scratch,
        compiler_params=pltpu.CompilerParams(
            dimension_semantics=("arbitrary", "arbitrary"), vmem_limit_bytes=min(vmem, VMEM_BYTES - (6 << 20))),
        name="ffn_to_time_major" if to_time_major else "ffn_final",
    )(x, norm_g, w1, w3, w2, final_g)


def _mixer_kernel(x_ref, gn_ref, win_ref, are_ref, aim_ref, bm_ref, cm_ref, d_ref, gluw_ref, glub_ref,
                  sow_ref, cw_ref, cb_ref, lng_ref, lnb_ref, cow_ref, wo_ref, h0_ref, cache_ref,
                  o_ref, st_ref, cbuf_ref,
                  hb_scr, u_scr, bu_scr, vp_scr, conv_scr, cn_scr, gate_scr, y_scr, z_scr, part_scr, mix_scr,
                  *, nb, q):
    step = pl.program_id(0)
    rows = nb * q
    hist_rows = HIST * nb
    groups = nb // SUBLANES
    n_slabs = CONV_SLABS

    in_grid = step < pl.num_programs(0)

    def stage(fn):
        pl.when(in_grid)(fn)

    def conv_slab(s):
        taps = [jnp.broadcast_to(cw_ref[s, k:k + 1, :], (SUBLANES, LANES)) for k in range(CONV_WIDTH)]
        bias = jnp.broadcast_to(cb_ref[s], (SUBLANES, LANES))
        for bg in range(groups):
            acc = [bias] * q
            for t_in in range(q + HIST):
                x_in = vp_scr[s, t_in * nb + bg * SUBLANES:t_in * nb + (bg + 1) * SUBLANES, :]
                for t in range(max(0, t_in - HIST), min(q, t_in + 1)):
                    acc[t] = acc[t] + taps[t_in - t] * x_in
            for t in range(q):
                conv_scr[s, t * nb + bg * SUBLANES:t * nb + (bg + 1) * SUBLANES, :] = acc[t]

    def bu_piece(h):
        ub = u_scr[:, h * MXU_DIM:(h + 1) * MXU_DIM].astype(BF16)
        bu_scr[:, 2 * h * HALF_STATE:2 * (h + 1) * HALF_STATE] = _dot(ub, bm_ref[h])

    def glu_piece(c):
        conv_a =_dot(hb_scr[...], win_ref[:, D_SSM + c * MXU_DIM:D_SSM + (c + 1) * MXU_DIM])
        conv_g = _dot(hb_scr[...], win_ref[:, D_SSM + D_CONV + c * MXU_DIM:D_SSM + D_CONV + (c + 1) * MXU_DIM])
        v = conv_a * _sigmoid(conv_g)
        for s in range(MXU_DIM // LANES):
            vp_scr[c * (MXU_DIM // LANES) + s, hist_rows:hist_rows + rows, :] = v[:, s * LANES:(s + 1) * LANES]

    def y_piece(h, row_blocks):
        for r in [slice(i * rows // row_blocks, (i + 1) * rows // row_blocks) for i in range(row_blocks)]:
            s = bu_scr[r, 2 * h * HALF_STATE:2 * (h + 1) * HALF_STATE].astype(BF16)
            y_scr[r, h * MXU_DIM:(h + 1) * MXU_DIM] = _dot(s, cm_ref[h])

    def scan_pass(h, j):
        re_cols = slice(2 * h * HALF_STATE + j * SCAN_LANES, 2 * h * HALF_STATE + (j + 1) * SCAN_LANES)
        im_cols = slice(re_cols.start + HALF_STATE, re_cols.stop + HALF_STATE)
        a_cols = slice(h * HALF_STATE + j * SCAN_LANES, h * HALF_STATE + (j + 1) * SCAN_LANES)
        ar = jnp.broadcast_to(are_ref[:, a_cols], (SUBLANES, SCAN_LANES))
        ai = jnp.broadcast_to(aim_ref[:, a_cols], (SUBLANES, SCAN_LANES))
        state = [(st_ref[bg * SUBLANES:(bg + 1) * SUBLANES, re_cols],
                  st_ref[bg * SUBLANES:(bg + 1) * SUBLANES, im_cols]) for bg in range(groups)]
        for t in range(q):
            for bg in range(groups):
                sr, si = state[bg]
                r = slice(t * nb + bg * SUBLANES, t * nb + (bg + 1) * SUBLANES)
                nr = ar * sr - ai * si + bu_scr[r, re_cols]
                ni = ar * si + ai * sr + bu_scr[r, im_cols]
                bu_scr[r, re_cols] = nr
                bu_scr[r, im_cols] = ni
                state[bg] = (nr, ni)
        for bg in range(groups):
            st_ref[bg * SUBLANES:(bg + 1) * SUBLANES, re_cols] = state[bg][0]
            st_ref[bg * SUBLANES:(bg + 1) * SUBLANES, im_cols] = state[bg][1]

    @pl.when(step == 0)
    def _():
        st_ref[...] = h0_ref[...]
        for s in range(n_slabs):
            vp_scr[s, 0:hist_rows, :] = cache_ref[:, s * LANES:(s + 1) * LANES]

    hb_scr[...] = _rms(x_ref[...], gn_ref[...]).astype(BF16)
    u_scr[...] = _dot(hb_scr[...], win_ref[:, 0:D_SSM])
    bu_piece(0)

    @stage
    def _():
        scan_pass(0, 0)
        bu_piece(1)

    @stage
    def _():
        scan_pass(0, 1)
        glu_piece(0)

    @stage
    def _():
        scan_pass(1, 0)
        glu_piece(1)

    @stage
    def _():
        scan_pass(1, 1)
        y_piece(0, 1)

    def conv_body(s, carry):
        conv_slab(s)
        return carry

    lax.fori_loop(0, n_slabs, conv_body, 0)

    @pl.when(step == pl.num_programs(0) - 1)
    def _():
        for s in range(n_slabs):
            cbuf_ref[:, s * LANES:(s + 1) * LANES] = vp_scr[s, rows:rows + hist_rows, :]

    for s in range(n_slabs):
        for r in range(0, hist_rows, nb):
            vp_scr[s, r:r + nb, :] = vp_scr[s, rows + r:rows + r + nb, :]

    ln_rows = rows // N_GATE_CHUNKS
    for c in range(N_GATE_CHUNKS):
        gate_cols = slice(N_PROJ + c * MXU_DIM, N_PROJ + (c + 1) * MXU_DIM)
        gate_scr[c] = _sigmoid(_dot(hb_scr[...], win_ref[:, gate_cols]))
        r = slice(c * ln_rows, (c + 1) * ln_rows)
        cv = jnp.concatenate([conv_scr[s, r, :] for s in range(n_slabs)], axis=-1)
        xc = cv - jnp.mean(cv, axis=-1, keepdims=True)
        var = jnp.mean(xc * xc, axis=-1, keepdims=True)
        ln = xc * lax.rsqrt(var + EPS) * lng_ref[...] + lnb_ref[...]
        cn_scr[r, :] = (ln * _sigmoid(ln)).astype(BF16)

    y_piece(1, 2)
    n_out = D_MODEL // MXU_DIM
    for c in range(n_out):
        cols = slice(c * MXU_DIM, (c + 1) * MXU_DIM)
        part_scr[:, cols] = gate_scr[n_out + c] * _dot(cn_scr[...], cow_ref[:, cols])
        r = slice(c * rows // n_out, (c + 1) * rows // n_out)
        y = y_scr[r, :] + d_ref[...] * u_scr[r, :]
        z_scr[r, :] = 0.5 * y * (1.0 + jnp.tanh(math.sqrt(2.0 / math.pi) * (y + 0.044715 * (y * y * y))))

    z = z_scr[...]
    zb = (z * _sigmoid(_dot(z.astype(BF16), gluw_ref[...]) + glub_ref[...])).astype(BF16)
    for c in range(n_out):
        cols = slice(c * MXU_DIM, (c + 1) * MXU_DIM)
        mix_scr[:, cols] = (gate_scr[c] * _dot(zb, sow_ref[:, cols]) + part_scr[:, cols]).astype(BF16)
    o_ref[...] = x_ref[...] + _dot(mix_scr[...], wo_ref[...])


def _mixer(x_tm, p, h0, cache_tm, *, nb, length, q):
    rows = nb * q
    assert nb % SUBLANES == 0 and N_GATE_CHUNKS % CONV_SLABS == 0
    consts = [
        (p["mix_norm"], (1, D_MODEL)), (p["w_in"], (D_MODEL, N_PROJ + N_GATE_CHUNKS * MXU_DIM)),
        (p["a_re"], (1, N_STATE)), (p["a_im"], (1, N_STATE)),
        (p["bm"], (SSM_HALVES, MXU_DIM, 2 * HALF_STATE)), (p["cm"], (SSM_HALVES, 2 * HALF_STATE, MXU_DIM)),
        (p["ssm_d"], (1, D_SSM)), (p["ssm_glu_w"], (D_SSM, D_SSM)), (p["ssm_glu_b"], (1, D_SSM)),
        (p["ssm_out_w"], (D_SSM, D_MODEL)),
        (p["conv_dw_w"], (CONV_SLABS, CONV_WIDTH, LANES)), (p["conv_dw_b"], (CONV_SLABS, 1, LANES)),
        (p["conv_ln_g"], (1, D_CONV)), (p["conv_ln_b"], (1, D_CONV)), (p["conv_out_w"], (D_CONV, D_MODEL)),
        (p["w_o"], (D_MODEL, D_MODEL)),
        (h0, (nb, 2 * N_STATE)), (cache_tm, (HIST * nb, D_CONV)),
    ]
    x_spec = pl.BlockSpec((rows, D_MODEL), lambda i: (i, 0))
    scratch_shapes = [
        ((rows, D_MODEL), BF16),
        ((rows, D_SSM), F32),
        ((rows, 2 * N_STATE), F32),
        ((CONV_SLABS, (HIST + q) * nb, LANES), F32),
        ((CONV_SLABS, rows, LANES), F32),
        ((rows, D_CONV), BF16),
        ((N_GATE_CHUNKS, rows, MXU_DIM), F32),
        ((rows, D_SSM), F32),
        ((rows, D_SSM), F32),
        ((rows, D_MODEL), F32),
        ((rows, D_MODEL), BF16),
    ]
    const_bytes = sum(math.prod(s) * a.dtype.itemsize for a, s in consts)
    scratch_bytes = sum(math.prod(s) * jnp.dtype(d).itemsize for s, d in scratch_shapes)
    block_bytes = rows * D_MODEL * 4
    vmem = const_bytes + scratch_bytes + 4 * block_bytes + 6 * block_bytes + (4 << 20)
    return pl.pallas_call(
        functools.partial(_mixer_kernel, nb=nb, q=q),
        grid=(length // q,),
        in_specs=[x_spec] + [_const_spec(s) for _, s in consts],
        out_specs=(x_spec, pl.BlockSpec((nb, 2 * N_STATE), lambda i: (0, 0)),
                   pl.BlockSpec((HIST * nb, D_CONV), lambda i: (0, 0))),
        out_shape=(jax.ShapeDtypeStruct((length * nb, D_MODEL), F32),
                   jax.ShapeDtypeStruct((nb, 2 * N_STATE), F32),
                   jax.ShapeDtypeStruct((HIST * nb, D_CONV), F32)),
        scratch_shapes=[pltpu.VMEM(s, d) for s, d in scratch_shapes],
        compiler_params=pltpu.CompilerParams(
            dimension_semantics=("arbitrary",), vmem_limit_bytes=min(vmem, VMEM_BYTES - (6 << 20))),
        name="mixer",
    )(x_tm, *[a for a, _ in consts])


def _trunk(x, p, final_g, h0, cache_tm, *, tq, q):
    nb, length, _ = x.shape
    x1 = _ffn(x, p["ffn1_norm"], p["ffn1_w1"], p["ffn1_w3"], p["ffn1_w2"], final_g,
              nb=nb, length=length, tq=tq, to_time_major=True)
    x2, state, conv_tail = _mixer(x1.reshape(length * nb, D_MODEL), p, h0, cache_tm, nb=nb, length=length, q=q)
    y = _ffn(x2.reshape(length, nb, D_MODEL), p["ffn2_norm"], p["ffn2_w1"], p["ffn2_w3"], p["ffn2_w2"], final_g,
             nb=nb, length=length, tq=tq, to_time_major=False)
    st = state.reshape(nb, SSM_HALVES, 2, SSM_GROUPS // SSM_HALVES, SSM_STATE)
    new_re = st[:, :, 0].reshape(1, nb, SSM_GROUPS, SSM_STATE)
    new_im = st[:, :, 1].reshape(1, nb, SSM_GROUPS, SSM_STATE)
    new_conv = conv_tail.reshape(HIST, nb, D_CONV).transpose(1, 0, 2)[None]
    return y, new_re, new_im, new_conv


def _state_rows(re, im):
    nb = re.shape[0]
    gph = SSM_GROUPS // SSM_HALVES
    st = jnp.stack([re.reshape(nb, SSM_HALVES, gph, SSM_STATE), im.reshape(nb, SSM_HALVES, gph, SSM_STATE)], axis=2)
    return st.reshape(nb, 2 * N_STATE).astype(F32)


def kernel(x_prompt, x_sample, state_ssm_re, state_ssm_im, cache_conv, ffn1_norm, ffn1_w1, ffn1_w3, ffn1_w2, mix_norm, w_in, ssm_lambda_re, ssm_lambda_im, ssm_log_step, ssm_b_re, ssm_b_im, ssm_c_re, ssm_c_im, ssm_d, ssm_glu_w, ssm_glu_b, ssm_out_w, conv_dw_w, conv_dw_b, conv_ln_g, conv_ln_b, conv_out_w, w_o, ffn2_norm, ffn2_w1, ffn2_w3, ffn2_w2, final_norm):
    a_re, a_im, bbt_re, bbt_im = _discretize(ssm_lambda_re[0], ssm_lambda_im[0], ssm_log_step[0],
                                             ssm_b_re[0], ssm_b_im[0])
    row = lambda v: v.reshape(1, -1).astype(F32)
    p = dict(
        ffn1_norm=row(ffn1_norm[0]), ffn1_w1=ffn1_w1[0].astype(BF16), ffn1_w3=ffn1_w3[0].astype(BF16),
        ffn1_w2=(FFN_RES * ffn1_w2[0]).astype(BF16),
        mix_norm=row(mix_norm[0]), w_in=w_in[0].astype(BF16),
        a_re=a_re, a_im=a_im, bm=_block_diag_b(bbt_re, bbt_im), cm=_block_diag_c(ssm_c_re[0], ssm_c_im[0]),
        ssm_d=row(ssm_d[0]), ssm_glu_w=ssm_glu_w[0].astype(BF16), ssm_glu_b=row(ssm_glu_b[0]),
        ssm_out_w=ssm_out_w[0].astype(BF16),
        conv_dw_w=conv_dw_w[0].astype(F32).reshape(CONV_WIDTH, CONV_SLABS, LANES).transpose(1, 0, 2),
        conv_dw_b=conv_dw_b[0].astype(F32).reshape(CONV_SLABS, 1, LANES),
        conv_ln_g=row(conv_ln_g[0]), conv_ln_b=row(conv_ln_b[0]), conv_out_w=conv_out_w[0].astype(BF16),
        w_o=w_o[0].astype(BF16),
        ffn2_norm=row(ffn2_norm[0]), ffn2_w1=ffn2_w1[0].astype(BF16), ffn2_w3=ffn2_w3[0].astype(BF16),
        ffn2_w2=(FFN_RES * ffn2_w2[0]).astype(BF16),
    )
    final_g = row(final_norm)

    nb_p, len_p, _ = x_prompt.shape
    y_p, re_p, im_p, conv_p = _trunk(
        x_prompt, p, final_g, jnp.zeros((nb_p, 2 * N_STATE), F32), jnp.zeros((HIST * nb_p, D_CONV), F32),
        tq=128, q=16)

    nb_s, len_s, _ = x_sample.shape
    cache_tm = cache_conv[0].astype(F32).transpose(1, 0, 2).reshape(HIST * nb_s, D_CONV)
    y_s, re_s, im_s, conv_s = _trunk(
        x_sample, p, final_g, _state_rows(state_ssm_re[0], state_ssm_im[0]), cache_tm, tq=len_s, q=len_s)

    return (y_p, y_s, re_p, im_p, conv_p, re_s, im_s, conv_s)
```

```python
import functools
import math

import jax
import jax.numpy as jnp
from jax import lax
from jax.experimental import pallas as pl
from jax.experimental.pallas import tpu as pltpu

D_MODEL = 1024
D_SSM = 512
SSM_GROUP = 16
SSM_GROUPS = 32
SSM_STATE = 64
D_CONV = 512
CONV_WIDTH = 31
D_FF = 2816
FFN_RES = 0.5
EPS = 1e-6

SUBLANES = 8
LANES = 128
MXU_DIM = 256
VMEM_BYTES = 64 * 1024 * 1024

N_STATE = SSM_GROUPS * SSM_STATE
SSM_HALVES = D_SSM // MXU_DIM
HALF_STATE = N_STATE // SSM_HALVES
SCAN_LANES = 512
CONV_SLABS = D_CONV // LANES
HIST = CONV_WIDTH - 1
N_PROJ = D_SSM + 2 * D_CONV
N_GATE_CHUNKS = 2 * D_MODEL // MXU_DIM

F32 = jnp.float32
BF16 = jnp.bfloat16


def _sigmoid(x):
    return 1.0 / (1.0 + jnp.exp(-x))


def _rms(x, g):
    return x * lax.rsqrt(jnp.mean(x * x, axis=-1, keepdims=True) + EPS) * g


def _dot(a, b):
    return jnp.dot(a, b, preferred_element_type=F32)


def _const_spec(shape):
    return pl.BlockSpec(shape, lambda *_: (0,) * len(shape), pipeline_mode=pl.Buffered(1))


def _discretize_kernel(lre_ref, lim_ref, dt_ref, bre_ref, bim_ref, are_ref, aim_ref, bbre_ref, bbim_ref):
    lre = lre_ref[...]
    lim = lim_ref[...]
    dt = jnp.exp(dt_ref[...])
    mag = jnp.exp(lre * dt)
    a_re = mag * jnp.cos(lim * dt)
    a_im = mag * jnp.sin(lim * dt)
    num_re = a_re - 1.0
    inv_den = 1.0 / (lre * lre + lim * lim)
    k_re = (num_re * lre + a_im * lim) * inv_den
    k_im = (a_im * lre - num_re * lim) * inv_den
    are_ref[...] = a_re
    aim_ref[...] = a_im
    bre = bre_ref[...]
    bim = bim_ref[...]
    bbre_ref[...] = k_re * bre - k_im * bim
    bbim_ref[...] = k_re * bim + k_im * bre


def _discretize(lam_re, lam_im, log_step, b_re, b_im):
    lre = lam_re.reshape(1, N_STATE).astype(F32)
    lim = lam_im.reshape(1, N_STATE).astype(F32)
    dt = jnp.repeat(log_step.astype(F32), SSM_STATE).reshape(1, N_STATE)
    bre = b_re.astype(F32).transpose(2, 0, 1).reshape(SSM_GROUP, N_STATE)
    bim = b_im.astype(F32).transpose(2, 0, 1).reshape(SSM_GROUP, N_STATE)
    row = jax.ShapeDtypeStruct((1, N_STATE), F32)
    mat = jax.ShapeDtypeStruct((SSM_GROUP, N_STATE), F32)
    return pl.pallas_call(_discretize_kernel, out_shape=(row, row, mat, mat), name="s5_discretize")(
        lre, lim, dt, bre, bim)


def _block_diag_b(bbt_re, bbt_im):
    gph = SSM_GROUPS // SSM_HALVES
    eye = jnp.eye(gph, dtype=F32)

    def one(bbt):
        b = bbt.reshape(SSM_GROUP, SSM_HALVES, gph, SSM_STATE)
        m = jnp.einsum("chgp,gk->hgckp", b, eye)
        return m.reshape(SSM_HALVES, MXU_DIM, HALF_STATE)

    return jnp.concatenate([one(bbt_re), one(bbt_im)], axis=-1).astype(BF16)


def _block_diag_c(c_re, c_im):
    gph = SSM_GROUPS // SSM_HALVES
    eye = jnp.eye(gph, dtype=F32)

    def one(c):
        cc = c.astype(F32).reshape(SSM_HALVES, gph, SSM_GROUP, SSM_STATE)
        m = jnp.einsum("hgcp,gk->hgpkc", cc, eye)
        return m.reshape(SSM_HALVES, HALF_STATE, MXU_DIM)

    return jnp.concatenate([one(c_re), -one(c_im)], axis=1).astype(BF16)


def _ffn_kernel(x_ref, g_ref, w1_ref, w3_ref, w2_ref, gf_ref, o_ref, *scratch, ff_chunk, to_time_major):
    if to_time_major:
        xbuf, sems = scratch
        tq = xbuf.shape[1]
        g, i = pl.program_id(0), pl.program_id(1)
        n_i = pl.num_programs(1)
        step = g * n_i + i
        last = pl.num_programs(0) * n_i - 1
        slot = lax.rem(step, 2)

        def tile_copies(gg, ii, ss):
            return [pltpu.make_async_copy(x_ref.at[gg * SUBLANES + b, pl.ds(ii * tq, tq), :],
                                          xbuf.at[ss, :, b, :], sems.at[ss, b]) for b in range(SUBLANES)]

        @pl.when(step == 0)
        def _():
            for cp in tile_copies(g, i, slot):
                cp.start()

        @pl.when(step < last)
        def _():
            wrap = i + 1 == n_i
            for cp in tile_copies(jnp.where(wrap, g + 1, g), jnp.where(wrap, 0, i + 1), 1 - slot):
                cp.start()

        for cp in tile_copies(g, i, slot):
            cp.wait()
        xb = xbuf[slot]
        n0, n1 = tq, SUBLANES
    else:
        ybuf, osems = scratch
        xb = x_ref[...]
        n0, n1, _ = xb.shape
        tq = n0
        g, i = pl.program_id(0), pl.program_id(1)
        n_i = pl.num_programs(1)
        step = g * n_i + i
        last = pl.num_programs(0) * n_i - 1
        slot = lax.rem(step, 2)

        def out_copies(ss):
            return [pltpu.make_async_copy(ybuf.at[ss, :, b, :],
                                          o_ref.at[g * SUBLANES + b, pl.ds(i * tq, tq), :], osems.at[ss, b])
                    for b in range(SUBLANES)]

        @pl.when(step >= 2)
        def _():
            for cp in out_copies(slot):
                cp.wait()
    x = xb.reshape(n0 * n1, D_MODEL)
    h = _rms(x, g_ref[...]).astype(BF16)
    acc = jnp.zeros(x.shape, F32)
    for c in range(D_FF // ff_chunk):
        cols = slice(c * ff_chunk, (c + 1) * ff_chunk)
        a = _dot(h, w1_ref[:, cols])
        b = _dot(h, w3_ref[:, cols])
        gated = (a * _sigmoid(a) * b).astype(BF16)
        acc = acc + _dot(gated, w2_ref[cols, :])
    y = x + acc
    if to_time_major:
        o_ref[...] = y.reshape(n0, n1, D_MODEL)
        return
    ybuf[slot] = _rms(y, gf_ref[...]).reshape(n0, n1, D_MODEL)
    for cp in out_copies(slot):
        cp.start()

    @pl.when(step == last)
    def _():
        for cp in out_copies(slot):
            cp.wait()

    @pl.when(jnp.logical_and(step == last, last >= 1))
    def _():
        for cp in out_copies(1 - slot):
            cp.wait()


def _ffn(x, norm_g, w1, w3, w2, final_g, *, nb, length, tq, to_time_major):
    batch_major = pl.BlockSpec((SUBLANES, tq, D_MODEL), lambda g, i: (g, i, 0))
    time_major = pl.BlockSpec((tq, SUBLANES, D_MODEL), lambda g, i: (i, g, 0))
    if to_time_major:
        in_spec, out_spec = pl.BlockSpec(memory_space=pl.ANY), time_major
        out_shape = jax.ShapeDtypeStruct((length, nb, D_MODEL), F32)
        scratch = [pltpu.VMEM((2, tq, SUBLANES, D_MODEL), F32),
                   pltpu.SemaphoreType.DMA((2, SUBLANES))]
    else:
        in_spec, out_spec = time_major, pl.BlockSpec(memory_space=pl.ANY)
        out_shape = jax.ShapeDtypeStruct((nb, length, D_MODEL), F32)
        scratch = [pltpu.VMEM((2, tq, SUBLANES, D_MODEL), F32),
                   pltpu.SemaphoreType.DMA((2, SUBLANES))]
    weight_bytes = 3 * D_MODEL * D_FF * 2
    block_bytes = SUBLANES * tq * D_MODEL * 4
    vmem = weight_bytes + 4 * block_bytes + 10 * block_bytes + (8 << 20)
    return pl.pallas_call(
        functools.partial(_ffn_kernel, ff_chunk=MXU_DIM, to_time_major=to_time_major),
        grid=(nb // SUBLANES, length // tq),
        in_specs=[in_spec, _const_spec((1, D_MODEL)), _const_spec((D_MODEL, D_FF)),
                  _const_spec((D_MODEL, D_FF)), _const_spec((D_FF, D_MODEL)), _const_spec((1, D_MODEL))],
        out_specs=out_spec,
        out_shape=out_shape,
        scratch_shapes=scratch,
        compiler_params=pltpu.CompilerParams(
            dimension_semantics=("arbitrary", "arbitrary"), vmem_limit_bytes=min(vmem, VMEM_BYTES - (6 << 20))),
        name="ffn_to_time_major" if to_time_major else "ffn_final",
    )(x, norm_g, w1, w3, w2, final_g)


def _mixer_kernel(x_ref, gn_ref, win_ref, are_ref, aim_ref, bm_ref, cm_ref, d_ref, gluw_ref, glub_ref,
                  sow_ref, cw_ref, cb_ref, lng_ref, lnb_ref, cow_ref, wo_ref, h0_ref, cache_ref,
                  o_ref, st_ref, cbuf_ref,
                  hb_scr, u_scr, bu_scr, vp_scr, conv_scr, cn_scr, gate_scr, y_scr, z_scr, part_scr, mix_scr,
                  *, nb, q):
    step = pl.program_id(0)
    rows = nb * q
    hist_rows = HIST * nb
    groups = nb // SUBLANES
    n_slabs = CONV_SLABS

    in_grid = step < pl.num_programs(0)

    def stage(fn):
        pl.when(in_grid)(fn)

    def conv_slab(s):
        taps = [jnp.broadcast_to(cw_ref[s, k:k + 1, :], (SUBLANES, LANES)) for k in range(CONV_WIDTH)]
        bias = jnp.broadcast_to(cb_ref[s], (SUBLANES, LANES))
        for bg in range(groups):
            acc = [bias] * q
            for t_in in range(q + HIST):
                x_in = vp_scr[s, t_in * nb + bg * SUBLANES:t_in * nb + (bg + 1) * SUBLANES, :]
                for t in range(max(0, t_in - HIST), min(q, t_in + 1)):
                    acc[t] = acc[t] + taps[t_in - t] * x_in
            for t in range(q):
                conv_scr[s, t * nb + bg * SUBLANES:t * nb + (bg + 1) * SUBLANES, :] = acc[t]

    def bu_piece(h):
        ub = u_scr[:, h * MXU_DIM:(h + 1) * MXU_DIM].astype(BF16)
        bu_scr[:, 2 * h * HALF_STATE:2 * (h + 1) * HALF_STATE] = _dot(ub, bm_ref[h])

    def glu_piece(c):
        conv_a =_dot(hb_scr[...], win_ref[:, D_SSM + c * MXU_DIM:D_SSM + (c + 1) * MXU_DIM])
        conv_g = _dot(hb_scr[...], win_ref[:, D_SSM + D_CONV + c * MXU_DIM:D_SSM + D_CONV + (c + 1) * MXU_DIM])
        v = conv_a * _sigmoid(conv_g)
        for s in range(MXU_DIM // LANES):
            vp_scr[c * (MXU_DIM // LANES) + s, hist_rows:hist_rows + rows, :] = v[:, s * LANES:(s + 1) * LANES]

    def y_piece(h, row_blocks):
        for r in [slice(i * rows // row_blocks, (i + 1) * rows // row_blocks) for i in range(row_blocks)]:
            s = bu_scr[r, 2 * h * HALF_STATE:2 * (h + 1) * HALF_STATE].astype(BF16)
            y_scr[r, h * MXU_DIM:(h + 1) * MXU_DIM] = _dot(s, cm_ref[h])

    def scan_pass(h, j):
        re_cols = slice(2 * h * HALF_STATE + j * SCAN_LANES, 2 * h * HALF_STATE + (j + 1) * SCAN_LANES)
        im_cols = slice(re_cols.start + HALF_STATE, re_cols.stop + HALF_STATE)
        a_cols = slice(h * HALF_STATE + j * SCAN_LANES, h * HALF_STATE + (j + 1) * SCAN_LANES)
        ar = jnp.broadcast_to(are_ref[:, a_cols], (SUBLANES, SCAN_LANES))
        ai = jnp.broadcast_to(aim_ref[:, a_cols], (SUBLANES, SCAN_LANES))
        state = [(st_ref[bg * SUBLANES:(bg + 1) * SUBLANES, re_cols],
                  st_ref[bg * SUBLANES:(bg + 1) * SUBLANES, im_cols]) for bg in range(groups)]
        for t in range(q):
            for bg in range(groups):
                sr, si = state[bg]
                r = slice(t * nb + bg * SUBLANES, t * nb + (bg + 1) * SUBLANES)
                nr = ar * sr - ai * si + bu_scr[r, re_cols]
                ni = ar * si + ai * sr + bu_scr[r, im_cols]
                bu_scr[r, re_cols] = nr
                bu_scr[r, im_cols] = ni
                state[bg] = (nr, ni)
        for bg in range(groups):
            st_ref[bg * SUBLANES:(bg + 1) * SUBLANES, re_cols] = state[bg][0]
            st_ref[bg * SUBLANES:(bg + 1) * SUBLANES, im_cols] = state[bg][1]

    @pl.when(step == 0)
    def _():
        st_ref[...] = h0_ref[...]
        for s in range(n_slabs):
            vp_scr[s, 0:hist_rows, :] = cache_ref[:, s * LANES:(s + 1) * LANES]

    hb_scr[...] = _rms(x_ref[...], gn_ref[...]).astype(BF16)
    u_scr[...] = _dot(hb_scr[...], win_ref[:, 0:D_SSM])
    bu_piece(0)

    @stage
    def _():
        scan_pass(0, 0)
        bu_piece(1)

    @stage
    def _():
        scan_pass(0, 1)
        glu_piece(0)

    @stage
    def _():
        scan_pass(1, 0)
        glu_piece(1)

    @stage
    def _():
        scan_pass(1, 1)
        y_piece(0, 1)

    def conv_body(s, carry):
        conv_slab(s)
        return carry

    lax.fori_loop(0, n_slabs, conv_body, 0)

    @pl.when(step == pl.num_programs(0) - 1)
    def _():
        for s in range(n_slabs):
            cbuf_ref[:, s * LANES:(s + 1) * LANES] = vp_scr[s, rows:rows + hist_rows, :]

    for s in range(n_slabs):
        for r in range(0, hist_rows, nb):
            vp_scr[s, r:r + nb, :] = vp_scr[s, rows + r:rows + r + nb, :]

    ln_rows = rows // N_GATE_CHUNKS
    for c in range(N_GATE_CHUNKS):
        gate_cols = slice(N_PROJ + c * MXU_DIM, N_PROJ + (c + 1) * MXU_DIM)
        gate_scr[c] = _sigmoid(_dot(hb_scr[...], win_ref[:, gate_cols]))
        r = slice(c * ln_rows, (c + 1) * ln_rows)
        cv = jnp.concatenate([conv_scr[s, r, :] for s in range(n_slabs)], axis=-1)
        xc = cv - jnp.mean(cv, axis=-1, keepdims=True)
        var = jnp.mean(xc * xc, axis=-1, keepdims=True)
        ln = xc * lax.rsqrt(var + EPS) * lng_ref[...] + lnb_ref[...]
        cn_scr[r, :] = (ln * _sigmoid(ln)).astype(BF16)

    y_piece(1, 2)
    n_out = D_MODEL // MXU_DIM
    for c in range(n_out):
        cols = slice(c * MXU_DIM, (c + 1) * MXU_DIM)
        part_scr[:, cols] = gate_scr[n_out + c] * _dot(cn_scr[...], cow_ref[:, cols])
        r = slice(c * rows // n_out, (c + 1) * rows // n_out)
        y = y_scr[r, :] + d_ref[...] * u_scr[r, :]
        z_scr[r, :] = 0.5 * y * (1.0 + jnp.tanh(math.sqrt(2.0 / math.pi) * (y + 0.044715 * (y * y * y))))

    z = z_scr[...]
    zb = (z * _sigmoid(_dot(z.astype(BF16), gluw_ref[...]) + glub_ref[...])).astype(BF16)
    for c in range(n_out):
        cols = slice(c * MXU_DIM, (c + 1) * MXU_DIM)
        mix_scr[:, cols] = (gate_scr[c] * _dot(zb, sow_ref[:, cols]) + part_scr[:, cols]).astype(BF16)
    o_ref[...] = x_ref[...] + _dot(mix_scr[...], wo_ref[...])


def _mixer(x_tm, p, h0, cache_tm, *, nb, length, q):
    rows = nb * q
    assert nb % SUBLANES == 0 and N_GATE_CHUNKS % CONV_SLABS == 0
    consts = [
        (p["mix_norm"], (1, D_MODEL)), (p["w_in"], (D_MODEL, N_PROJ + N_GATE_CHUNKS * MXU_DIM)),
        (p["a_re"], (1, N_STATE)), (p["a_im"], (1, N_STATE)),
        (p["bm"], (SSM_HALVES, MXU_DIM, 2 * HALF_STATE)), (p["cm"], (SSM_HALVES, 2 * HALF_STATE, MXU_DIM)),
        (p["ssm_d"], (1, D_SSM)), (p["ssm_glu_w"], (D_SSM, D_SSM)), (p["ssm_glu_b"], (1, D_SSM)),
        (p["ssm_out_w"], (D_SSM, D_MODEL)),
        (p["conv_dw_w"], (CONV_SLABS, CONV_WIDTH, LANES)), (p["conv_dw_b"], (CONV_SLABS, 1, LANES)),
        (p["conv_ln_g"], (1, D_CONV)), (p["conv_ln_b"], (1, D_CONV)), (p["conv_out_w"], (D_CONV, D_MODEL)),
        (p["w_o"], (D_MODEL, D_MODEL)),
        (h0, (nb, 2 * N_STATE)), (cache_tm, (HIST * nb, D_CONV)),
    ]
    x_spec = pl.BlockSpec((rows, D_MODEL), lambda i: (i, 0))
    scratch_shapes = [
        ((rows, D_MODEL), BF16),
        ((rows, D_SSM), F32),
        ((rows, 2 * N_STATE), F32),
        ((CONV_SLABS, (HIST + q) * nb, LANES), F32),
        ((CONV_SLABS, rows, LANES), F32),
        ((rows, D_CONV), BF16),
        ((N_GATE_CHUNKS, rows, MXU_DIM), F32),
        ((rows, D_SSM), F32),
        ((rows, D_SSM), F32),
        ((rows, D_MODEL), F32),
        ((rows, D_MODEL), BF16),
    ]
    const_bytes = sum(math.prod(s) * a.dtype.itemsize for a, s in consts)
    scratch_bytes = sum(math.prod(s) * jnp.dtype(d).itemsize for s, d in scratch_shapes)
    block_bytes = rows * D_MODEL * 4
    vmem = const_bytes + scratch_bytes + 4 * block_bytes + 6 * block_bytes + (4 << 20)
    return pl.pallas_call(
        functools.partial(_mixer_kernel, nb=nb, q=q),
        grid=(length // q,),
        in_specs=[x_spec] + [_const_spec(s) for _, s in consts],
        out_specs=(x_spec, pl.BlockSpec((nb, 2 * N_STATE), lambda i: (0, 0)),
                   pl.BlockSpec((HIST * nb, D_CONV), lambda i: (0, 0))),
        out_shape=(jax.ShapeDtypeStruct((length * nb, D_MODEL), F32),
                   jax.ShapeDtypeStruct((nb, 2 * N_STATE), F32),
                   jax.ShapeDtypeStruct((HIST * nb, D_CONV), F32)),
        scratch_shapes=[pltpu.VMEM(s, d) for s, d in scratch_shapes],
        compiler_params=pltpu.CompilerParams(
            dimension_semantics=("arbitrary",), vmem_limit_bytes=min(vmem, VMEM_BYTES - (6 << 20))),
        name="mixer",
    )(x_tm, *[a for a, _ in consts])


def _trunk(x, p, final_g, h0, cache_tm, *, tq, q):
    nb, length, _ = x.shape
    x1 = _ffn(x, p["ffn1_norm"], p["ffn1_w1"], p["ffn1_w3"], p["ffn1_w2"], final_g,
              nb=nb, length=length, tq=tq, to_time_major=True)
    x2, state, conv_tail = _mixer(x1.reshape(length * nb, D_MODEL), p, h0, cache_tm, nb=nb, length=length, q=q)
    y = _ffn(x2.reshape(length, nb, D_MODEL), p["ffn2_norm"], p["ffn2_w1"], p["ffn2_w3"], p["ffn2_w2"], final_g,
             nb=nb, length=length, tq=tq, to_time_major=False)
    st = state.reshape(nb, SSM_HALVES, 2, SSM_GROUPS // SSM_HALVES, SSM_STATE)
    new_re = st[:, :, 0].reshape(1, nb, SSM_GROUPS, SSM_STATE)
    new_im = st[:, :, 1].reshape(1, nb, SSM_GROUPS, SSM_STATE)
    new_conv = conv_tail.reshape(HIST, nb, D_CONV).transpose(1, 0, 2)[None]
    return y, new_re, new_im, new_conv


def _state_rows(re, im):
    nb = re.shape[0]
    gph = SSM_GROUPS // SSM_HALVES
    st = jnp.stack([re.reshape(nb, SSM_HALVES, gph, SSM_STATE), im.reshape(nb, SSM_HALVES, gph, SSM_STATE)], axis=2)
    return st.reshape(nb, 2 * N_STATE).astype(F32)


def kernel(x_prompt, x_sample, state_ssm_re, state_ssm_im, cache_conv, ffn1_norm, ffn1_w1, ffn1_w3, ffn1_w2, mix_norm, w_in, ssm_lambda_re, ssm_lambda_im, ssm_log_step, ssm_b_re, ssm_b_im, ssm_c_re, ssm_c_im, ssm_d, ssm_glu_w, ssm_glu_b, ssm_out_w, conv_dw_w, conv_dw_b, conv_ln_g, conv_ln_b, conv_out_w, w_o, ffn2_norm, ffn2_w1, ffn2_w3, ffn2_w2, final_norm):
    a_re, a_im, bbt_re, bbt_im = _discretize(ssm_lambda_re[0], ssm_lambda_im[0], ssm_log_step[0],
                                             ssm_b_re[0], ssm_b_im[0])
    row = lambda v: v.reshape(1, -1).astype(F32)
    p = dict(
        ffn1_norm=row(ffn1_norm[0]), ffn1_w1=ffn1_w1[0].astype(BF16), ffn1_w3=ffn1_w3[0].astype(BF16),
        ffn1_w2=(FFN_RES * ffn1_w2[0]).astype(BF16),
        mix_norm=row(mix_norm[0]), w_in=w_in[0].astype(BF16),
        a_re=a_re, a_im=a_im, bm=_block_diag_b(bbt_re, bbt_im), cm=_block_diag_c(ssm_c_re[0], ssm_c_im[0]),
        ssm_d=row(ssm_d[0]), ssm_glu_w=ssm_glu_w[0].astype(BF16), ssm_glu_b=row(ssm_glu_b[0]),
        ssm_out_w=ssm_out_w[0].astype(BF16),
        conv_dw_w=conv_dw_w[0].astype(F32).reshape(CONV_WIDTH, CONV_SLABS, LANES).transpose(1, 0, 2),
        conv_dw_b=conv_dw_b[0].astype(F32).reshape(CONV_SLABS, 1, LANES),
        conv_ln_g=row(conv_ln_g[0]), conv_ln_b=row(conv_ln_b[0]), conv_out_w=conv_out_w[0].astype(BF16),
        w_o=w_o[0].astype(BF16),
        ffn2_norm=row(ffn2_norm[0]), ffn2_w1=ffn2_w1[0].astype(BF16), ffn2_w3=ffn2_w3[0].astype(BF16),
        ffn2_w2=(FFN_RES * ffn2_w2[0]).astype(BF16),
    )
    final_g = row(final_norm)

    nb_p, len_p, _ = x_prompt.shape
    y_p, re_p, im_p, conv_p = _trunk(
        x_prompt, p, final_g, jnp.zeros((nb_p, 2 * N_STATE), F32), jnp.zeros((HIST * nb_p, D_CONV), F32),
        tq=128, q=16)

    nb_s, len_s, _ = x_sample.shape
    cache_tm = cache_conv[0].astype(F32).transpose(1, 0, 2).reshape(HIST * nb_s, D_CONV)
    y_s, re_s, im_s, conv_s = _trunk(
        x_sample, p, final_g, _state_rows(state_ssm_re[0], state_ssm_im[0]), cache_tm, tq=len_s, q=len_s)

    return (y_p, y_s, re_p, im_p, conv_p, re_s, im_s, conv_s)
```
